```python
import math
import jax, jax.numpy as jnp
from jax import lax
import numpy as np

D_MODEL = 4096
BATCH = 4
SEQ = 2048
DEPTH = 4
DEC_BATCH = 128
DEC_SEQ = 8
PAST_LEN = 16384
PAGE_SIZE = 128

D_MIX = D_MODEL
D_POOL = D_MIX // 4
POOL_WINDOWS = (2, 4, 8, 16)
N_POOL_GROUPS = len(POOL_WINDOWS)
POOL_GW = D_POOL // N_POOL_GROUPS
POOL_BUF = max(POOL_WINDOWS) - 1
D_GLA_V = D_MIX - D_POOL
GLA_DV = 512
GLA_HEADS = D_GLA_V // GLA_DV
GLA_DK = GLA_DV // 2
D_GLA_K = GLA_HEADS * GLA_DK
GATE_RANK = 16
GATE_NORM = 16.0
GLA_CHUNK = 64
OFF_Q = D_POOL
OFF_K = OFF_Q + D_GLA_K
OFF_V = OFF_K + D_GLA_K
OFF_R = OFF_V + D_GLA_V
OFF_G = OFF_R + D_GLA_V
D_IN = OFF_G + GATE_RANK
N_EXPERTS = 16
N_EXPERT_GROUPS = 4
EXPERTS_PER_GROUP = N_EXPERTS // N_EXPERT_GROUPS
TOP_K = 2
D_EXPERT = D_MODEL // 4
N_MOD = 6
EPS = 1e-6

kernel_name = "hybrid_pool_gla_moe_adaln_decode_step"


def rmsnorm(x, g):
    xf = x.astype(jnp.float32)
    y = xf * lax.rsqrt(jnp.mean(xf * xf, axis=-1, keepdims=True) + EPS) * g.astype(jnp.float32)
    return y.astype(x.dtype)


def pool_mix(u, u_prev, pos0, pool_w, pool_scale):
    T = u.shape[1]
    P = POOL_BUF
    u_ext = jnp.concatenate([u_prev.astype(jnp.float32), u.astype(jnp.float32)], axis=1)
    csum = jnp.concatenate([jnp.zeros_like(u_ext[:, :1]), jnp.cumsum(u_ext, axis=1)], axis=1)
    end = csum[:, P + 1:]
    cur = u_ext[:, P:]
    pos = pos0 + jnp.arange(T)
    outs = []
    for g, w in enumerate(POOL_WINDOWS):
        sl = slice(g * POOL_GW, (g + 1) * POOL_GW)
        start = csum[:, P + 1 - w:P + 1 - w + T, sl]
        cnt = jnp.minimum(pos + 1, w).astype(jnp.float32)[None, :, None]
        d = (end[..., sl] - start) / cnt - cur[..., sl]
        outs.append(jnp.einsum('btc,cd->btd', d, pool_w[g].astype(jnp.float32)))
    y = jnp.concatenate(outs, axis=-1) * pool_scale.astype(jnp.float32)
    return y.astype(u.dtype), u_ext[:, -P:].astype(u.dtype)


def gla_chunked(q, k, v, log_a, s0):
    Bq, T, H, K = q.shape
    V = v.shape[-1]
    C = math.gcd(T, GLA_CHUNK)
    N = T // C
    q = q.reshape(Bq, N, C, H, K)
    k = k.reshape(Bq, N, C, H, K)
    v = v.reshape(Bq, N, C, H, V)
    b = jnp.cumsum(log_a.reshape(Bq, N, C, H, K), axis=2)
    b_ref = b[:, :, C // 2][:, :, None]
    b_last = b[:, :, -1]
    a_intra = jnp.einsum('bnihk,bnjhk->bnhij', q * jnp.exp(b - b_ref), k * jnp.exp(b_ref - b))
    causal = jnp.tril(jnp.ones((C, C), dtype=bool))
    a_intra = jnp.where(causal, a_intra, 0.0)
    o_intra = jnp.einsum('bnhij,bnjhv->bnihv', a_intra, v)
    q_inter = q * jnp.exp(b)
    k_carry = k * jnp.exp(b_last[:, :, None] - b)
    decay = jnp.exp(b_last)

    def step(s, xs):
        qn, kn, vn, dn = xs
        o = jnp.einsum('bchk,bhkv->bchv', qn, s)
        s = dn[..., None] * s + jnp.einsum('bchk,bchv->bhkv', kn, vn)
        return s, o

    xs = (jnp.moveaxis(q_inter, 1, 0), jnp.moveaxis(k_carry, 1, 0),
          jnp.moveaxis(v, 1, 0), jnp.moveaxis(decay, 1, 0))
    s_fin, o_inter = lax.scan(step, s0, xs)
    o = o_intra + jnp.moveaxis(o_inter, 0, 1)
    return o.reshape(Bq, T, H, V), s_fin


def moe(h, router_w, router_b, w1, w3, w2):
    Bh, T, D = h.shape
    hf = h.reshape(-1, D)
    n_tok = hf.shape[0]
    scores = jax.nn.sigmoid((hf @ router_w).astype(jnp.float32))
    sel = scores + router_b.astype(jnp.float32)
    grp_score = lax.top_k(sel.reshape(n_tok, N_EXPERT_GROUPS, EXPERTS_PER_GROUP), 2)[0].sum(-1)
    top_g = jnp.argmax(grp_score, axis=-1)
    in_grp = (jnp.arange(N_EXPERTS) // EXPERTS_PER_GROUP)[None, :] == top_g[:, None]
    _, idx = lax.top_k(jnp.where(in_grp, sel, -jnp.inf), TOP_K)
    wts = jnp.take_along_axis(scores, idx, axis=-1)
    wts = wts / jnp.sum(wts, axis=-1, keepdims=True)
    combine = jnp.sum(jax.nn.one_hot(idx, N_EXPERTS, dtype=jnp.float32) * wts[..., None], axis=1)
    out = jnp.zeros((n_tok, D), jnp.float32)
    for e in range(N_EXPERTS):
        y = (jax.nn.silu(hf @ w1[e]) * (hf @ w3[e])) @ w2[e]
        out = out + combine[:, e:e + 1] * y.astype(jnp.float32)
    return out.reshape(Bh, T, D).astype(h.dtype)


def trunk(x, c, pool_prev, gla_prev, pos0, ada_w, ada_b, norm1_g, norm2_g, w_in, pool_w, pool_scale,
          gk_w2, gk_b, gla_norm_g, w_out, router_w, router_b, moe_w1, moe_w3, moe_w2, final_g):
    Bx, T, _ = x.shape
    pool_states, gla_states = [], []
    for l in range(DEPTH):
        mod = jax.nn.silu(c) @ ada_w[l] + ada_b[l]
        sh1, sc1, g1, sh2, sc2, g2 = [m[:, None, :] for m in jnp.split(mod, N_MOD, axis=-1)]
        h = rmsnorm(x, norm1_g[l]) * (1 + sc1) + sh1
        z = h @ w_in[l]
        pool_y, pool_new = pool_mix(z[..., :OFF_Q], pool_prev[l], pos0, pool_w[l], pool_scale[l])
        q = z[..., OFF_Q:OFF_K].astype(jnp.float32).reshape(Bx, T, GLA_HEADS, GLA_DK) * GLA_DK ** -0.5
        k = z[..., OFF_K:OFF_V].astype(jnp.float32).reshape(Bx, T, GLA_HEADS, GLA_DK)
        v = z[..., OFF_V:OFF_R].astype(jnp.float32).reshape(Bx, T, GLA_HEADS, GLA_DV)
        r = z[..., OFF_R:OFF_G].astype(jnp.float32).reshape(Bx, T, GLA_HEADS, GLA_DV)
        gate_logit = (z[..., OFF_G:] @ gk_w2[l] + gk_b[l]).astype(jnp.float32)
        log_a = jax.nn.log_sigmoid(gate_logit).reshape(Bx, T, GLA_HEADS, GLA_DK) / GATE_NORM
        o, s_new = gla_chunked(q, k, v, log_a, gla_prev[l].astype(jnp.float32))
        o = rmsnorm(o, gla_norm_g[l]) * jax.nn.silu(r)
        mixed = jnp.concatenate([pool_y, o.reshape(Bx, T, D_GLA_V).astype(x.dtype)], axis=-1) @ w_out[l]
        x = x + g1 * mixed
        h2 = rmsnorm(x, norm2_g[l]) * (1 + sc2) + sh2
        x = x + g2 * moe(h2, router_w, router_b, moe_w1[l], moe_w3[l], moe_w2[l])
        pool_states.append(pool_new)
        gla_states.append(s_new.astype(x.dtype))
    return rmsnorm(x, final_g), jnp.stack(pool_states), jnp.stack(gla_states)


def setup_inputs(seed: int = 0) -> dict:
    key = jax.random.key(seed)
    ks = jax.random.split(key, 24)

    def nrm(k, shape, scale):
        return jax.random.normal(k, shape, jnp.float32) * scale

    return {
        "x_prompt": nrm(ks[0], (BATCH, SEQ, D_MODEL), 1.0),
        "x_sample": nrm(ks[1], (DEC_BATCH, DEC_SEQ, D_MODEL), 1.0),
        "c_prompt": nrm(ks[2], (BATCH, D_MODEL), 1.0),
        "c_sample": nrm(ks[3], (DEC_BATCH, D_MODEL), 1.0),
        "state_pool": nrm(ks[4], (DEPTH, DEC_BATCH, POOL_BUF, D_POOL), 1.0),
        "state_gla": nrm(ks[5], (DEPTH, DEC_BATCH, GLA_HEADS, GLA_DK, GLA_DV), 0.5),
        "ada_w": nrm(ks[6], (DEPTH, D_MODEL, N_MOD * D_MODEL), 0.5 * D_MODEL ** -0.5),
        "ada_b": nrm(ks[7], (DEPTH, N_MOD * D_MODEL), 0.02),
        "norm1_g": 1.0 + nrm(ks[8], (DEPTH, D_MODEL), 0.02),
        "norm2_g": 1.0 + nrm(ks[9], (DEPTH, D_MODEL), 0.02),
        "w_in": nrm(ks[10], (DEPTH, D_MODEL, D_IN), D_MODEL ** -0.5),
        "pool_w": nrm(ks[11], (DEPTH, N_POOL_GROUPS, POOL_GW, POOL_GW), POOL_GW ** -0.5),
        "pool_scale": 1.0 + nrm(ks[12], (DEPTH, D_POOL), 0.1),
        "gk_w2": nrm(ks[13], (DEPTH, GATE_RANK, D_GLA_K), GATE_RANK ** -0.5),
        "gk_b": nrm(ks[14], (DEPTH, D_GLA_K), 0.1),
        "gla_norm_g": 1.0 + nrm(ks[15], (DEPTH, GLA_DV), 0.02),
        "w_out": nrm(ks[16], (DEPTH, D_MIX, D_MODEL), D_MIX ** -0.5),
        "router_w": nrm(ks[17], (D_MODEL, N_EXPERTS), D_MODEL ** -0.5),
        "router_b": nrm(ks[18], (N_EXPERTS,), 0.01),
        "moe_w1": nrm(ks[19], (DEPTH, N_EXPERTS, D_MODEL, D_EXPERT), D_MODEL ** -0.5),
        "moe_w3": nrm(ks[20], (DEPTH, N_EXPERTS, D_MODEL, D_EXPERT), D_MODEL ** -0.5),
        "moe_w2": nrm(ks[21], (DEPTH, N_EXPERTS, D_EXPERT, D_MODEL), D_EXPERT ** -0.5),
        "final_g": 1.0 + nrm(ks[22], (D_MODEL,), 0.02),
    }


def reference(x_prompt, x_sample, c_prompt, c_sample, state_pool, state_gla, ada_w, ada_b, norm1_g,
              norm2_g, w_in, pool_w, pool_scale, gk_w2, gk_b, gla_norm_g, w_out, router_w, router_b,
              moe_w1, moe_w3, moe_w2, final_g):
    weights = (ada_w, ada_b, norm1_g, norm2_g, w_in, pool_w, pool_scale, gk_w2, gk_b, gla_norm_g,
               w_out, router_w, router_b, moe_w1, moe_w3, moe_w2, final_g)
    bp = x_prompt.shape[0]
    pool0 = jnp.zeros((DEPTH, bp, POOL_BUF, D_POOL), x_prompt.dtype)
    gla0 = jnp.zeros((DEPTH, bp, GLA_HEADS, GLA_DK, GLA_DV), x_prompt.dtype)
    y_prompt, pool_p, gla_p = trunk(x_prompt, c_prompt, pool0, gla0, 0, *weights)
    y_sample, pool_s, gla_s = trunk(x_sample, c_sample, state_pool, state_gla, PAST_LEN, *weights)
    return (y_prompt, y_sample, pool_p, gla_p, pool_s, gla_s)
```

```python
import functools

import jax
import jax.numpy as jnp
from jax import lax
from jax.experimental import pallas as pl
from jax.experimental.pallas import tpu as pltpu

F32 = jnp.float32
BF16 = jnp.bfloat16

D_MODEL = 4096
BATCH = 4
SEQ = 2048
DEPTH = 4
DEC_BATCH = 128
DEC_SEQ = 8
PAST_LEN = 16384
D_POOL = 1024
POOL_WINDOWS = (2, 4, 8, 16)
POOL_GW = 256
POOL_BUF = 15
D_GLA_V = 3072
GLA_DV = 512
GLA_HEADS = 6
GLA_DK = 256
D_GLA_K = 1536
GATE_RANK = 16
GATE_NORM = 16.0
GLA_CHUNK = 64
OFF_Q = D_POOL
OFF_K = OFF_Q + D_GLA_K
OFF_V = OFF_K + D_GLA_K
OFF_R = OFF_V + D_GLA_V
OFF_G = OFF_R + D_GLA_V
N_EXPERTS = 16
N_EXPERT_GROUPS = 4
EXPERTS_PER_GROUP = 4
TOP_K = 2
D_EXPERT = 1024
N_MOD = 6
EPS = 1e-6

N_P = BATCH * SEQ
N_S = DEC_BATCH * DEC_SEQ
N_TOK = N_P + N_S
N_COND = DEC_BATCH + BATCH

TT = 128
N_PT = N_P // TT
N_ST = N_S // TT
TILES_PER_SEQ = SEQ // TT
MM_TM = 1024
MM_TN = 512
POOL_T = 256
POOL_HIST = 16
MOE_TM = 512
MOE_TF = 256
MOE_TN = 2048
N_SLOT_TILES = (TOP_K * N_TOK) // MOE_TM + N_EXPERTS
N_SLOTS = N_SLOT_TILES * MOE_TM
GLA_SEQ_BLK = 8

VMEM_MIB = 1024 * 1024


def _params(semantics, vmem_mib):
    return pltpu.CompilerParams(dimension_semantics=semantics, vmem_limit_bytes=vmem_mib * VMEM_MIB)


def _split3(x):
    hi = x.astype(BF16)
    r = x - hi.astype(F32)
    mid = r.astype(BF16)
    lo = (r - mid.astype(F32)).astype(BF16)
    return hi, mid, lo


def _dot(a, b):
    return jnp.dot(a, b, preferred_element_type=F32)


def _dot_nt(a, b):
    return lax.dot_general(a, b, (((1,), (1,)), ((), ())), preferred_element_type=F32)


def _dot_tn(a, b):
    return lax.dot_general(a, b, (((0,), (0,)), ((), ())), preferred_element_type=F32)


def _rms(x, g):
    return x * lax.rsqrt(jnp.mean(x * x, axis=-1, keepdims=True) + EPS) * g


def _silu(x):
    return x * jax.nn.sigmoid(x)


def _by_tile_kind(i, tabs, body):
    @pl.when(i < N_PT)
    def _():
        row = DEC_BATCH + i // TILES_PER_SEQ
        body([t[pl.ds(row, 1), :] for t in tabs])

    @pl.when(i >= N_PT)
    def _():
        body([t[0:DEC_BATCH, :] for t in tabs])


def _tab_spec(layer, j):
    return pl.BlockSpec((None, N_COND, D_MODEL), lambda i, *_: (layer, 0, j))


def _row_spec(width):
    return pl.BlockSpec((TT, width), lambda i, *_: (i, 0))


def _vec_spec(layer):
    return pl.BlockSpec((None, 1, D_MODEL), lambda i, *_: (layer, 0, 0))


def _mod_kernel(c_ref, w_ref, b_ref, o_ref):
    a = _silu(c_ref[...]).astype(BF16)
    o_ref[...] = _dot(a, w_ref[...].astype(BF16)) + b_ref[...]


def _adaln(c_all, ada_w, ada_b):
    tn = 1024
    n = ada_w.shape[-1]
    return pl.pallas_call(
        _mod_kernel,
        grid=(DEPTH, n // tn),
        in_specs=[
            pl.BlockSpec((N_COND, D_MODEL), lambda l, j: (0, 0)),
            pl.BlockSpec((None, D_MODEL, tn), lambda l, j: (l, 0, j)),
            pl.BlockSpec((None, 1, tn), lambda l, j: (l, 0, j)),
        ],
        out_specs=pl.BlockSpec((None, N_COND, tn), lambda l, j: (l, 0, j)),
        out_shape=jax.ShapeDtypeStruct((DEPTH, N_COND, n), F32),
        compiler_params=_params(("arbitrary", "arbitrary"), 52),
        name="adaln",
    )(c_all, ada_w, ada_b.reshape(DEPTH, 1, n))


def _norm_mod_kernel(x_ref, g_ref, sc_ref, sh_ref, h_ref):
    def body(mods):
        sc, sh = mods
        h_ref[...] = (_rms(x_ref[...], g_ref[...]) * (1 + sc) + sh).astype(h_ref.dtype)

    _by_tile_kind(pl.program_id(0), [sc_ref, sh_ref], body)


def _norm_mod(x, mod, norm_g, layer):
    return pl.pallas_call(
        _norm_mod_kernel,
        grid=(N_PT + N_ST,),
        in_specs=[_row_spec(D_MODEL), _vec_spec(layer), _tab_spec(layer, 1), _tab_spec(layer, 0)],
        out_specs=_row_spec(D_MODEL),
        out_shape=jax.ShapeDtypeStruct((N_TOK, D_MODEL), BF16),
        compiler_params=_params(("arbitrary",), 32),
        name="norm_mod",
    )(x, norm_g.reshape(DEPTH, 1, D_MODEL), mod, mod)


def _post_mix_kernel(x_ref, m_ref, g_ref, gate_ref, sc_ref, sh_ref, x1_ref, h2_ref):
    def body(mods):
        gate, sc, sh = mods
        x1 = x_ref[...] + gate * m_ref[...]
        x1_ref[...] = x1
        h2_ref[...] = _rms(x1, g_ref[...]) * (1 + sc) + sh

    _by_tile_kind(pl.program_id(0), [gate_ref, sc_ref, sh_ref], body)


def _post_mix(x, mixed, mod, norm_g, layer):
    return pl.pallas_call(
        _post_mix_kernel,
        grid=(N_PT + N_ST,),
        in_specs=[_row_spec(D_MODEL), _row_spec(D_MODEL), _vec_spec(layer),
                  _tab_spec(layer, 2), _tab_spec(layer, 4), _tab_spec(layer, 3)],
        out_specs=[_row_spec(D_MODEL), _row_spec(D_MODEL)],
        out_shape=[jax.ShapeDtypeStruct((N_TOK, D_MODEL), F32)] * 2,
        compiler_params=_params(("arbitrary",), 40),
        name="post_mix",
    )(x, mixed, norm_g.reshape(DEPTH, 1, D_MODEL), mod, mod, mod)


def _mm_kernel(a_ref, w_ref, o_ref, wb_ref):
    @pl.when(pl.program_id(1) == 0)
    def _():
        wb_ref[...] = w_ref[...].astype(BF16)

    o_ref[...] = _dot(a_ref[...], wb_ref[...]).astype(o_ref.dtype)


def _matmul(a, w, layer, n_cols, name):
    m, k = a.shape
    return pl.pallas_call(
        _mm_kernel,
        grid=(n_cols // MM_TN, m // MM_TM),
        in_specs=[
            pl.BlockSpec((MM_TM, k), lambda j, i: (i, 0)),
            pl.BlockSpec((None, k, MM_TN), lambda j, i: (layer, 0, j)),
        ],
        out_specs=pl.BlockSpec((MM_TM, MM_TN), lambda j, i: (i, j)),
        out_shape=jax.ShapeDtypeStruct((m, n_cols), F32),
        scratch_shapes=[pltpu.VMEM((k, MM_TN), BF16)],
        compiler_params=_params(("arbitrary", "arbitrary"), 48),
        name=name,
    )(a, w)


def _gate_kernel(h_ref, wg_ref, w2_ref, b_ref, o_ref):
    zg = _dot(h_ref[...], wg_ref[...].astype(BF16))
    z_hi, z_mid, _ = _split3(zg)
    w_hi, w_mid, _ = _split3(w2_ref[...])
    logit = _dot(z_hi, w_hi) + _dot(z_hi, w_mid) + _dot(z_mid, w_hi) + b_ref[...]
    o_ref[...] = (jnp.minimum(logit, 0.0) - jnp.log1p(jnp.exp(-jnp.abs(logit)))) * (1.0 / GATE_NORM)


def _gate(h, w_gate, gk_w2, gk_b, layer):
    tm = 512
    return pl.pallas_call(
        _gate_kernel,
        grid=(N_TOK // tm,),
        in_specs=[
            pl.BlockSpec((tm, D_MODEL), lambda i: (i, 0)),
            pl.BlockSpec((D_MODEL, GATE_RANK), lambda i: (0, 0)),
            pl.BlockSpec((None, GATE_RANK, D_GLA_K), lambda i: (layer, 0, 0)),
            pl.BlockSpec((None, 1, D_GLA_K), lambda i: (layer, 0, 0)),
        ],
        out_specs=pl.BlockSpec((tm, D_GLA_K), lambda i: (i, 0)),
        out_shape=jax.ShapeDtypeStruct((N_TOK, D_GLA_K), F32),
        compiler_params=_params(("arbitrary",), 32),
        name="gla_gate",
    )(h, w_gate, gk_w2, gk_b.reshape(DEPTH, 1, D_GLA_K))


def _pool_windows(load, pos, pw_ref, ps_ref, o_ref):
    for g, w in enumerate(POOL_WINDOWS):
        cs = slice(g * POOL_GW, (g + 1) * POOL_GW)
        cur = load(0, cs)
        win = cur
        for k in range(1, w):
            win = win + load(k, cs)
        cnt = jnp.minimum(pos + 1, w).astype(F32)
        d = win / cnt - cur
        y = _dot(d.astype(BF16), pw_ref[g].astype(BF16))
        o_ref[:, cs] = (y * ps_ref[:, cs]).astype(o_ref.dtype)


def _pool_prompt_kernel(u_ref, pw_ref, ps_ref, o_ref, ext_ref):
    c = pl.program_id(1)

    @pl.when(c == 0)
    def _():
        ext_ref[0:POOL_HIST, :] = jnp.zeros((POOL_HIST, D_POOL), F32)

    @pl.when(c > 0)
    def _():
        ext_ref[0:POOL_HIST, :] = ext_ref[POOL_T:POOL_T + POOL_HIST, :]

    ext_ref[POOL_HIST:POOL_HIST + POOL_T, :] = u_ref[...]
    pos = c * POOL_T + lax.broadcasted_iota(jnp.int32, (POOL_T, POOL_GW), 0)

    def load(k, cs):
        return ext_ref[POOL_HIST - k:POOL_HIST - k + POOL_T, cs]

    _pool_windows(load, pos, pw_ref, ps_ref, o_ref)


def _pool_sample_kernel(u_ref, prev_ref, pw_ref, ps_ref, _mix_in, o_ref):
    def load(k, cs):
        outs = []
        for t in range(DEC_SEQ):
            p = t - k
            if p >= 0:
                outs.append(u_ref[p * DEC_BATCH:(p + 1) * DEC_BATCH, cs])
            else:
                outs.append(prev_ref[POOL_BUF + p, :, cs])
        return jnp.concatenate(outs, axis=0)

    row = lax.broadcasted_iota(jnp.int32, (N_S, POOL_GW), 0)
    pos = PAST_LEN + lax.shift_right_logical(row, DEC_BATCH.bit_length() - 1)
    _pool_windows(load, pos, pw_ref, ps_ref, o_ref)


def _pool_prompt(z, pool_w, pool_scale, layer):
    steps = SEQ // POOL_T
    return pl.pallas_call(
        _pool_prompt_kernel,
        grid=(BATCH, steps),
        in_specs=[
            pl.BlockSpec((POOL_T, D_POOL), lambda b, c: (b * steps + c, 0)),
            pl.BlockSpec((None, len(POOL_WINDOWS), POOL_GW, POOL_GW), lambda b, c: (layer, 0, 0, 0)),
            pl.BlockSpec((None, 1, D_POOL), lambda b, c: (layer, 0, 0)),
        ],
        out_specs=pl.BlockSpec((POOL_T, D_POOL), lambda b, c: (b * steps + c, 0)),
        out_shape=jax.ShapeDtypeStruct((N_TOK, D_MODEL), BF16),
        scratch_shapes=[pltpu.VMEM((POOL_HIST + POOL_T, D_POOL), F32)],
        compiler_params=_params(("arbitrary", "arbitrary"), 32),
        name="pool_prompt",
    )(z, pool_w, pool_scale.reshape(DEPTH, 1, D_POOL))


def _pool_sample(z, prev_t, pool_w, pool_scale, mix, layer):
    blk = N_P // N_S
    return pl.pallas_call(
        _pool_sample_kernel,
        grid=(1,),
        in_specs=[
            pl.BlockSpec((N_S, D_POOL), lambda i: (blk, 0)),
            pl.BlockSpec((POOL_BUF, DEC_BATCH, D_POOL), lambda i: (0, 0, 0)),
            pl.BlockSpec((None, len(POOL_WINDOWS), POOL_GW, POOL_GW), lambda i: (layer, 0, 0, 0)),
            pl.BlockSpec((None, 1, D_POOL), lambda i: (layer, 0, 0)),
            pl.BlockSpec(memory_space=pl.ANY),
        ],
        out_specs=pl.BlockSpec((N_S, D_POOL), lambda i: (blk, 0)),
        out_shape=jax.ShapeDtypeStruct((N_TOK, D_MODEL), BF16),
        input_output_aliases={4: 0},
        compiler_params=_params(("arbitrary",), 48),
        name="pool_sample",
    )(z, prev_t, pool_w, pool_scale.reshape(DEPTH, 1, D_POOL), mix)


def _gla_chunk(la, q, k, v, r, g, s_ref, nseq, seq_len):
    c = GLA_CHUNK
    mid, last = seq_len // 2, seq_len - 1

    def seq_pos(idx):
        if nseq == 1:
            return jnp.zeros_like(idx), idx
        return idx & (nseq - 1), lax.shift_right_logical(idx, nseq.bit_length() - 1)

    ri = lax.broadcasted_iota(jnp.int32, (3 * c, c), 0)
    cj = lax.broadcasted_iota(jnp.int32, (3 * c, c), 1)
    kind = lax.shift_right_logical(ri, c.bit_length() - 1)
    si, pi = seq_pos(ri & (c - 1))
    sj, pj = seq_pos(cj)
    limit = jnp.where(kind == 0, pi, jnp.where(kind == 1, mid, last))
    sums = jnp.where((si == sj) & (pj <= limit), 1.0, 0.0).astype(BF16)
    si, pi = seq_pos(lax.broadcasted_iota(jnp.int32, (c, c), 0))
    sj, pj = seq_pos(lax.broadcasted_iota(jnp.int32, (c, c), 1))
    causal = (si == sj) & (pj <= pi)

    la_hi, la_mid, la_lo = _split3(la)
    b3 = _dot(sums, la_hi) + _dot(sums, la_mid) + _dot(sums, la_lo)
    b, b_mid, b_last = b3[0:c], b3[c:2 * c], b3[2 * c:3 * c]

    rs, _ = seq_pos(lax.broadcasted_iota(jnp.int32, (c, 128), 0))
    ls = lax.broadcasted_iota(jnp.int32, (c, 128), 1)
    pick = jnp.where(rs == (ls & (nseq - 1)), 1.0, 0.0).astype(BF16)
    blc = _dot_tn(la_hi, pick) + _dot_tn(la_mid, pick) + _dot_tn(la_lo, pick)

    qs = q * (GLA_DK ** -0.5)
    a = _dot_nt((qs * jnp.exp(b - b_mid)).astype(BF16), (k * jnp.exp(b_mid - b)).astype(BF16))
    a = jnp.where(causal, a, 0.0)
    vb = v.astype(BF16)
    o = _dot(a.astype(BF16), vb)
    q_inter = (qs * jnp.exp(b)).astype(BF16)
    k_carry = k * jnp.exp(b_last - b)
    row_seq, _ = seq_pos(lax.broadcasted_iota(jnp.int32, (c, 1), 0))
    o_inter = None
    for s in range(nseq):
        st = s_ref[s]
        oi = _dot(q_inter, st.astype(BF16))
        o_inter = oi if o_inter is None else jnp.where(row_seq == s, oi, o_inter)
        ks = k_carry if nseq == 1 else jnp.where(row_seq == s, k_carry, 0.0)
        s_ref[s] = jnp.exp(blc[:, s:s + 1]) * st + _dot_tn(ks.astype(BF16), vb)
    o = o + o_inter
    return _rms(o, g) * _silu(r)


def _gla_prompt_kernel(q_ref, k_ref, v_ref, r_ref, la_ref, g_ref, *rest):
    o_ref, s_ref = rest[-2:]
    s_ref[...] = jnp.zeros(s_ref.shape, F32)

    def chunk(ci, carry):
        rows = pl.ds(pl.multiple_of(ci * GLA_CHUNK, GLA_CHUNK), GLA_CHUNK)
        o = _gla_chunk(la_ref[rows, :], q_ref[rows, :], k_ref[rows, :], v_ref[rows, :], r_ref[rows, :],
                       g_ref[...], s_ref, 1, GLA_CHUNK)
        o_ref[rows, :] = o.astype(o_ref.dtype)
        return carry

    lax.fori_loop(0, SEQ // GLA_CHUNK, chunk, 0)


def _gla_sample_kernel(q_ref, k_ref, v_ref, r_ref, la_ref, g_ref, s0_ref, *rest):
    o_ref, s_ref, o_scr = rest[-3:]
    sb = pl.program_id(1)
    s_ref[...] = s0_ref[...]
    base = pl.multiple_of(sb * GLA_SEQ_BLK, GLA_SEQ_BLK)

    def rows(ref):
        return jnp.concatenate(
            [ref[pl.ds(t * DEC_BATCH + base, GLA_SEQ_BLK), :] for t in range(DEC_SEQ)], axis=0)

    o = _gla_chunk(rows(la_ref), rows(q_ref), rows(k_ref), rows(v_ref), rows(r_ref),
                   g_ref[...], s_ref, GLA_SEQ_BLK, DEC_SEQ)
    for t in range(DEC_SEQ):
        o_scr[pl.ds(t * DEC_BATCH + base, GLA_SEQ_BLK), :] = o[t * GLA_SEQ_BLK:(t + 1) * GLA_SEQ_BLK]

    @pl.when(sb == DEC_BATCH // GLA_SEQ_BLK - 1)
    def _():
        o_ref[...] = o_scr[...].astype(o_ref.dtype)


def _gla_prompt(z, log_a, norm_g, mix, s_stack, layer):
    def spec(width, off):
        return pl.BlockSpec((SEQ, width), lambda b, h: (b, off // width + h))

    ins = [z, z, z, z, log_a, norm_g.reshape(DEPTH, 1, GLA_DV), mix]
    in_specs = [spec(GLA_DK, OFF_Q), spec(GLA_DK, OFF_K), spec(GLA_DV, OFF_V), spec(GLA_DV, OFF_R),
                spec(GLA_DK, 0), pl.BlockSpec((None, 1, GLA_DV), lambda b, h: (layer, 0, 0)),
                pl.BlockSpec(memory_space=pl.ANY)]
    aliases = {6: 0}
    if s_stack is not None:
        ins.append(s_stack)
        in_specs.append(pl.BlockSpec(memory_space=pl.ANY))
        aliases[7] = 1
    return pl.pallas_call(
        _gla_prompt_kernel,
        grid=(BATCH, GLA_HEADS),
        in_specs=in_specs,
        out_specs=[
            pl.BlockSpec((SEQ, GLA_DV), lambda b, h: (b, D_POOL // GLA_DV + h)),
            pl.BlockSpec((None, 1, None, GLA_DK, GLA_DV), lambda b, h: (layer, b, h, 0, 0)),
        ],
        out_shape=[jax.ShapeDtypeStruct((N_TOK, D_MODEL), BF16),
                   jax.ShapeDtypeStruct((DEPTH, BATCH, GLA_HEADS, GLA_DK, GLA_DV), F32)],
        input_output_aliases=aliases,
        compiler_params=_params(("arbitrary", "arbitrary"), 52),
        name="gla_prompt",
    )(*ins)


def _gla_sample(z, log_a, norm_g, state_gla, mix, s_stack, layer):
    blk = N_P // N_S

    def spec(width, off):
        return pl.BlockSpec((N_S, width), lambda h, sb: (blk, off // width + h))

    ins = [z, z, z, z, log_a, norm_g.reshape(DEPTH, 1, GLA_DV), state_gla, mix]
    in_specs = [spec(GLA_DK, OFF_Q), spec(GLA_DK, OFF_K), spec(GLA_DV, OFF_V), spec(GLA_DV, OFF_R),
                spec(GLA_DK, 0), pl.BlockSpec((None, 1, GLA_DV), lambda h, sb: (layer, 0, 0)),
                pl.BlockSpec((None, GLA_SEQ_BLK, None, GLA_DK, GLA_DV), lambda h, sb: (layer, sb, h, 0, 0)),
                pl.BlockSpec(memory_space=pl.ANY)]
    aliases = {7: 0}
    if s_stack is not None:
        ins.append(s_stack)
        in_specs.append(pl.BlockSpec(memory_space=pl.ANY))
        aliases[8] = 1
    return pl.pallas_call(
        _gla_sample_kernel,
        grid=(GLA_HEADS, DEC_BATCH // GLA_SEQ_BLK),
        in_specs=in_specs,
        out_specs=[
            pl.BlockSpec((N_S, GLA_DV), lambda h, sb: (blk, D_POOL // GLA_DV + h)),
            pl.BlockSpec((None, GLA_SEQ_BLK, None, GLA_DK, GLA_DV), lambda h, sb: (layer, sb, h, 0, 0)),
        ],
        out_shape=[jax.ShapeDtypeStruct((N_TOK, D_MODEL), BF16),
                   jax.ShapeDtypeStruct((DEPTH, DEC_BATCH, GLA_HEADS, GLA_DK, GLA_DV), F32)],
        scratch_shapes=[pltpu.VMEM((N_S, GLA_DV), F32)],
        input_output_aliases=aliases,
        compiler_params=_params(("arbitrary", "arbitrary"), 52),
        name="gla_sample",
    )(*ins)


def _router_kernel(h_ref, rwt_ref, rb_ref, comb_ref, mask_ref):
    h_hi, h_mid, _ = _split3(h_ref[...])
    w_hi, w_mid, _ = _split3(rwt_ref[...])
    logits = _dot_nt(w_hi, h_hi) + _dot_nt(w_hi, h_mid) + _dot_nt(w_mid, h_hi)
    scores = jax.nn.sigmoid(logits)
    sel = scores + rb_ref[...]
    a = [sel[e:e + 1] for e in range(N_EXPERTS)]
    sc = [scores[e:e + 1] for e in range(N_EXPERTS)]

    best, best_g = None, None
    for g in range(N_EXPERT_GROUPS):
        a0, a1, a2, a3 = a[4 * g:4 * g + 4]
        hi01, lo01 = jnp.maximum(a0, a1), jnp.minimum(a0, a1)
        hi23, lo23 = jnp.maximum(a2, a3), jnp.minimum(a2, a3)
        gs = jnp.maximum(hi01, hi23) + jnp.maximum(jnp.minimum(hi01, hi23), jnp.maximum(lo01, lo23))
        if g == 0:
            best, best_g = gs, jnp.zeros(gs.shape, jnp.int32)
        else:
            upd = gs > best
            best = jnp.where(upd, gs, best)
            best_g = jnp.where(upd, g, best_g)

    chosen = []
    for e in range(N_EXPERTS):
        g = e // EXPERTS_PER_GROUP
        rank = jnp.zeros(a[e].shape, jnp.int32)
        for j in range(4 * g, 4 * g + 4):
            if j == e:
                continue
            ahead = (a[j] > a[e]) | ((a[j] == a[e]) & (j < e))
            rank = rank + ahead.astype(jnp.int32)
        chosen.append((best_g == g) & (rank < TOP_K))

    denom = jnp.zeros(a[0].shape, F32)
    for e in range(N_EXPERTS):
        denom = denom + jnp.where(chosen[e], sc[e], 0.0)
    for e in range(N_EXPERTS):
        comb_ref[e:e + 1, :] = jnp.where(chosen[e], sc[e] / denom, 0.0)
        mask_ref[e:e + 1, :] = chosen[e].astype(F32)


def _router(h2, router_wt, router_b):
    tm = 256
    return pl.pallas_call(
        _router_kernel,
        grid=(N_TOK // tm,),
        in_specs=[
            pl.BlockSpec((tm, D_MODEL), lambda i: (i, 0)),
            pl.BlockSpec((N_EXPERTS, D_MODEL), lambda i: (0, 0)),
            pl.BlockSpec((N_EXPERTS, 1), lambda i: (0, 0)),
        ],
        out_specs=[pl.BlockSpec((N_EXPERTS, tm), lambda i: (0, i))] * 2,
        out_shape=[jax.ShapeDtypeStruct((N_EXPERTS, N_TOK), F32)] * 2,
        compiler_params=_params(("arbitrary",), 32),
        name="router",
    )(h2, router_wt, router_b.reshape(N_EXPERTS, 1))


def _route_indices(mask, comb):
    mb = mask > 0.5
    cnt = jnp.sum(mb, axis=1, dtype=jnp.int32)
    tiles_e = (cnt + MOE_TM - 1) // MOE_TM
    tile_end = jnp.cumsum(tiles_e)
    tile_start = tile_end - tiles_e
    slot = tile_start[:, None] * MOE_TM + jnp.cumsum(mb, axis=1, dtype=jnp.int32) - 1
    e0 = jnp.argmax(mb, axis=0)
    e1 = N_EXPERTS - 1 - jnp.argmax(mb[::-1], axis=0)
    tok = jnp.arange(N_TOK, dtype=jnp.int32)
    slot0 = slot[e0, tok]
    slot1 = slot[e1, tok]
    w0 = comb[e0, tok][:, None]
    w1 = comb[e1, tok][:, None]
    tok_of_slot = jnp.zeros((N_SLOTS,), jnp.int32).at[slot0].set(tok).at[slot1].set(tok)
    tile = jnp.arange(N_SLOT_TILES, dtype=jnp.int32)
    tile_expert = jnp.minimum(jnp.searchsorted(tile_end, tile, side="right"), N_EXPERTS - 1).astype(jnp.int32)
    tile_valid = (tile < tile_end[-1]).astype(jnp.int32)
    prev = jnp.concatenate([jnp.full((1,), -1, jnp.int32), tile_expert[:-1]])
    tile_new = (tile_expert != prev).astype(jnp.int32)
    return tok_of_slot, tile_expert, tile_valid, tile_new, slot0, slot1, w0, w1


def _gather_kernel(tok_ref, valid_ref, h_hbm, o_ref, buf, sem):
    t = pl.program_id(0)

    @pl.when(valid_ref[t] != 0)
    def _():
        base = t * MOE_TM

        def issue(r, carry):
            tok = tok_ref[base + r]
            pltpu.make_async_copy(h_hbm.at[pl.ds(tok, 1)], buf.at[pl.ds(r, 1)], sem.at[0]).start()
            return carry

        lax.fori_loop(0, MOE_TM, issue, 0, unroll=8)
        pltpu.make_async_copy(h_hbm.at[pl.ds(0, MOE_TM)], buf, sem.at[0]).wait()
        o_ref[...] = buf[...].astype(o_ref.dtype)

    @pl.when(valid_ref[t] == 0)
    def _():
        o_ref[...] = jnp.zeros(o_ref.shape, o_ref.dtype)


def _moe_gather(h2, tok_of_slot, tile_valid):
    return pl.pallas_call(
        _gather_kernel,
        grid_spec=pltpu.PrefetchScalarGridSpec(
            num_scalar_prefetch=2,
            grid=(N_SLOT_TILES,),
            in_specs=[pl.BlockSpec(memory_space=pl.ANY)],
            out_specs=pl.BlockSpec((MOE_TM, D_MODEL), lambda t, *_: (t, 0)),
            scratch_shapes=[pltpu.VMEM((MOE_TM, D_MODEL), F32), pltpu.SemaphoreType.DMA((1,))],
        ),
        out_shape=jax.ShapeDtypeStruct((N_SLOTS, D_MODEL), BF16),
        compiler_params=_params(("arbitrary",), 32),
        name="moe_gather",
    )(tok_of_slot, tile_valid, h2)


def _moe_up_kernel(te_ref, tv_ref, tn_ref, x_ref, w1_ref, w3_ref, o_ref, w1b, w3b):
    t = pl.program_id(1)

    @pl.when(tv_ref[t] != 0)
    def _():
        @pl.when(tn_ref[t] != 0)
        def _():
            w1b[...] = w1_ref[...].astype(BF16)
            w3b[...] = w3_ref[...].astype(BF16)

        x = x_ref[...]
        o_ref[...] = (_silu(_dot(x, w1b[...])) * _dot(x, w3b[...])).astype(o_ref.dtype)

    @pl.when(tv_ref[t] == 0)
    def _():
        o_ref[...] = jnp.zeros(o_ref.shape, o_ref.dtype)


def _moe_up(xs, w1, w3, tile_expert, tile_valid, tile_new, layer):
    def wspec():
        return pl.BlockSpec((None, None, D_MODEL, MOE_TF), lambda f, t, te, tv, tn: (layer, te[t], 0, f))

    return pl.pallas_call(
        _moe_up_kernel,
        grid_spec=pltpu.PrefetchScalarGridSpec(
            num_scalar_prefetch=3,
            grid=(D_EXPERT // MOE_TF, N_SLOT_TILES),
            in_specs=[pl.BlockSpec((MOE_TM, D_MODEL), lambda f, t, *_: (t, 0)), wspec(), wspec()],
            out_specs=pl.BlockSpec((MOE_TM, MOE_TF), lambda f, t, *_: (t, f)),
            scratch_shapes=[pltpu.VMEM((D_MODEL, MOE_TF), BF16)] * 2,
        ),
        out_shape=jax.ShapeDtypeStruct((N_SLOTS, D_EXPERT), BF16),
        compiler_params=_params(("arbitrary", "arbitrary"), 48),
        name="moe_up",
    )(tile_expert, tile_valid, tile_new, xs, w1, w3)


def _moe_down_kernel(te_ref, tv_ref, tn_ref, h_ref, w2_ref, o_ref, w2b):
    t = pl.program_id(1)

    @pl.when(tv_ref[t] != 0)
    def _():
        @pl.when(tn_ref[t] != 0)
        def _():
            w2b[...] = w2_ref[...].astype(BF16)

        o_ref[...] = _dot(h_ref[...], w2b[...])

    @pl.when(tv_ref[t] == 0)
    def _():
        o_ref[...] = jnp.zeros(o_ref.shape, o_ref.dtype)


def _moe_down(hid, w2, tile_expert, tile_valid, tile_new, layer):
    return pl.pallas_call(
        _moe_down_kernel,
        grid_spec=pltpu.PrefetchScalarGridSpec(
            num_scalar_prefetch=3,
            grid=(D_MODEL // MOE_TN, N_SLOT_TILES),
            in_specs=[
                pl.BlockSpec((MOE_TM, D_EXPERT), lambda n, t, *_: (t, 0)),
                pl.BlockSpec((None, None, D_EXPERT, MOE_TN), lambda n, t, te, tv, tn: (layer, te[t], 0, n)),
            ],
            out_specs=pl.BlockSpec((MOE_TM, MOE_TN), lambda n, t, *_: (t, n)),
            scratch_shapes=[pltpu.VMEM((D_EXPERT, MOE_TN), BF16)],
        ),
        out_shape=jax.ShapeDtypeStruct((N_SLOTS, D_MODEL), F32),
        compiler_params=_params(("arbitrary", "arbitrary"), 48),
        name="moe_down",
    )(tile_expert, tile_valid, tile_new, hid, w2)


def _combine_kernel(s0_ref, s1_ref, y_hbm, x_ref, w0_ref, w1_ref, gate_ref, g_ref, *rest,
                    tile0, final):
    if final:
        out_ref, buf0, buf1, sems = rest
        tabs = [gate_ref]
    else:
        sc_ref, sh_ref, x2_ref, h_ref, buf0, buf1, sems = rest
        tabs = [gate_ref, sc_ref, sh_ref]
    i = pl.program_id(0) + tile0
    base = i * TT

    def issue(r, carry):
        pltpu.make_async_copy(y_hbm.at[pl.ds(s0_ref[base + r], 1)], buf0.at[pl.ds(r, 1)], sems.at[0]).start()
        pltpu.make_async_copy(y_hbm.at[pl.ds(s1_ref[base + r], 1)], buf1.at[pl.ds(r, 1)], sems.at[1]).start()
        return carry

    lax.fori_loop(0, TT, issue, 0, unroll=8)
    pltpu.make_async_copy(y_hbm.at[pl.ds(0, TT)], buf0, sems.at[0]).wait()
    pltpu.make_async_copy(y_hbm.at[pl.ds(0, TT)], buf1, sems.at[1]).wait()

    def body(mods):
        moe = w0_ref[...] * buf0[...] + w1_ref[...] * buf1[...]
        x2 = x_ref[...] + mods[0] * moe
        if final:
            out_ref[...] = _rms(x2, g_ref[...])
        else:
            x2_ref[...] = x2
            h_ref[...] = (_rms(x2, g_ref[...]) * (1 + mods[1]) + mods[2]).astype(h_ref.dtype)

    _by_tile_kind(i, tabs, body)


def _combine(y, x1, slot0, slot1, w0, w1, mod, layer, norm_g, *, final, tile0=0, n_tiles=N_PT + N_ST):
    def rows(width):
        return pl.BlockSpec((TT, width), lambda i, *_: (i + tile0, 0))

    in_specs = [pl.BlockSpec(memory_space=pl.ANY), rows(D_MODEL), rows(1), rows(1), _tab_spec(layer, 5)]
    ins = [y, x1, w0, w1, mod]
    if final:
        in_specs.append(pl.BlockSpec((1, D_MODEL), lambda i, *_: (0, 0)))
        ins.append(norm_g.reshape(1, D_MODEL))
        out_specs = _row_spec(D_MODEL)
        out_shape = jax.ShapeDtypeStruct((n_tiles * TT, D_MODEL), F32)
    else:
        in_specs += [_vec_spec(layer + 1), _tab_spec(layer + 1, 1), _tab_spec(layer + 1, 0)]
        ins += [norm_g.reshape(DEPTH, 1, D_MODEL), mod, mod]
        out_specs = [_row_spec(D_MODEL), _row_spec(D_MODEL)]
        out_shape = [jax.ShapeDtypeStruct((N_TOK, D_MODEL), F32), jax.ShapeDtypeStruct((N_TOK, D_MODEL), BF16)]
    return pl.pallas_call(
        functools.partial(_combine_kernel, tile0=tile0, final=final),
        grid_spec=pltpu.PrefetchScalarGridSpec(
            num_scalar_prefetch=2,
            grid=(n_tiles,),
            in_specs=in_specs,
            out_specs=out_specs,
            scratch_shapes=[pltpu.VMEM((TT, D_MODEL), F32), pltpu.VMEM((TT, D_MODEL), F32),
                            pltpu.SemaphoreType.DMA((2,))],
        ),
        out_shape=out_shape,
        compiler_params=_params(("arbitrary",), 40),
        name="moe_combine_final" if final else "moe_combine",
    )(slot0, slot1, *ins)


def kernel(x_prompt, x_sample, c_prompt, c_sample, state_pool, state_gla, ada_w, ada_b, norm1_g, norm2_g, w_in, pool_w, pool_scale, gk_w2, gk_b, gla_norm_g, w_out, router_w, router_b, moe_w1, moe_w3, moe_w2, final_g):
    x = jnp.concatenate([x_prompt.reshape(N_P, D_MODEL),
                         x_sample.transpose(1, 0, 2).reshape(N_S, D_MODEL)], axis=0)
    mod = _adaln(jnp.concatenate([c_sample, c_prompt], axis=0), ada_w, ada_b)
    router_wt = router_w.T
    h = _norm_mod(x, mod, norm1_g, 0)

    pool_p, pool_s = [], []
    gla_p = gla_s = None
    for l in range(DEPTH):
        z = _matmul(h, w_in, l, OFF_G, "mm_in")
        log_a = _gate(h, w_in[l, :, OFF_G:], gk_w2, gk_b, l)

        prev_t = state_pool[l].transpose(1, 0, 2)
        mix = _pool_prompt(z, pool_w, pool_scale, l)
        mix = _pool_sample(z, prev_t, pool_w, pool_scale, mix, l)
        mix, gla_p = _gla_prompt(z, log_a, gla_norm_g, mix, gla_p, l)
        mix, gla_s = _gla_sample(z, log_a, gla_norm_g, state_gla, mix, gla_s, l)
        pool_p.append(jnp.stack([z[b * SEQ + SEQ - POOL_BUF:(b + 1) * SEQ, :D_POOL] for b in range(BATCH)]))
        u_s = z[N_P:, :D_POOL].reshape(DEC_SEQ, DEC_BATCH, D_POOL)
        pool_s.append(jnp.concatenate([prev_t[DEC_SEQ:], u_s], axis=0).transpose(1, 0, 2))

        mixed = _matmul(mix, w_out, l, D_MODEL, "mm_out")
        x1, h2 = _post_mix(x, mixed, mod, norm2_g, l)

        comb, mask = _router(h2, router_wt, router_b)
        tok_of_slot, tile_expert, tile_valid, tile_new, slot0, slot1, w0, w1 = _route_indices(mask, comb)
        xs = _moe_gather(h2, tok_of_slot, tile_valid)
        hid = _moe_up(xs, moe_w1, moe_w3, tile_expert, tile_valid, tile_new, l)
        y = _moe_down(hid, moe_w2, tile_expert, tile_valid, tile_new, l)
        if l + 1 < DEPTH:
            x, h = _combine(y, x1, slot0, slot1, w0, w1, mod, l, norm1_g, final=False)
        else:
            y_p = _combine(y, x1, slot0, slot1, w0, w1, mod, l, final_g, final=True, tile0=0, n_tiles=N_PT)
            y_s = _combine(y, x1, slot0, slot1, w0, w1, mod, l, final_g, final=True, tile0=N_PT, n_tiles=N_ST)

    return (y_p.reshape(BATCH, SEQ, D_MODEL),
            y_s.reshape(DEC_SEQ, DEC_BATCH, D_MODEL).transpose(1, 0, 2),
            jnp.stack(pool_p), gla_p, jnp.stack(pool_s), gla_s)
```

```python
import functools

import jax
import jax.numpy as jnp
from jax import lax
from jax.experimental import pallas as pl
from jax.experimental.pallas import tpu as pltpu

F32 = jnp.float32
BF16 = jnp.bfloat16

D_MODEL = 4096
BATCH = 4
SEQ = 2048
DEPTH = 4
DEC_BATCH = 128
DEC_SEQ = 8
PAST_LEN = 16384
D_POOL = 1024
POOL_WINDOWS = (2, 4, 8, 16)
POOL_GW = 256
POOL_BUF = 15
D_GLA_V = 3072
GLA_DV = 512
GLA_HEADS = 6
GLA_DK = 256
D_GLA_K = 1536
GATE_RANK = 16
GATE_NORM = 16.0
GLA_CHUNK = 64
OFF_Q = D_POOL
OFF_K = OFF_Q + D_GLA_K
OFF_V = OFF_K + D_GLA_K
OFF_R = OFF_V + D_GLA_V
OFF_G = OFF_R + D_GLA_V
N_EXPERTS = 16
N_EXPERT_GROUPS = 4
EXPERTS_PER_GROUP = 4
TOP_K = 2
D_EXPERT = 1024
N_MOD = 6
EPS = 1e-6

N_P = BATCH * SEQ
N_S = DEC_BATCH * DEC_SEQ
N_TOK = N_P + N_S
N_COND = DEC_BATCH + BATCH

TT = 128
N_PT = N_P // TT
N_ST = N_S // TT
TILES_PER_SEQ = SEQ // TT
MM_TM = 1024
MM_TN = 512
POOL_T = 256
POOL_HIST = 16
MOE_TM = 256
MOE_TF = 512
MOE_TN = 4096
N_SLOT_TILES = (TOP_K * N_TOK) // MOE_TM + N_EXPERTS
N_SLOTS = N_SLOT_TILES * MOE_TM
GLA_SEQ_BLK = 8

VMEM_MIB = 1024 * 1024


def _params(semantics, vmem_mib):
    return pltpu.CompilerParams(dimension_semantics=semantics, vmem_limit_bytes=vmem_mib * VMEM_MIB)


def _split3(x):
    hi = x.astype(BF16)
    r = x - hi.astype(F32)
    mid = r.astype(BF16)
    lo = (r - mid.astype(F32)).astype(BF16)
    return hi, mid, lo


def _dot(a, b):
    return jnp.dot(a, b, preferred_element_type=F32)


def _dot_nt(a, b):
    return lax.dot_general(a, b, (((1,), (1,)), ((), ())), preferred_element_type=F32)


def _dot_tn(a, b):
    return lax.dot_general(a, b, (((0,), (0,)), ((), ())), preferred_element_type=F32)


def _rms(x, g):
    return x * lax.rsqrt(jnp.mean(x * x, axis=-1, keepdims=True) + EPS) * g


def _silu(x):
    return x * jax.nn.sigmoid(x)


def _by_tile_kind(i, tabs, body):
    @pl.when(i < N_PT)
    def _():
        row = DEC_BATCH + i // TILES_PER_SEQ
        body([t[pl.ds(row, 1), :] for t in tabs])

    @pl.when(i >= N_PT)
    def _():
        body([t[0:DEC_BATCH, :] for t in tabs])


def _tab_spec(layer, j):
    return pl.BlockSpec((None, N_COND, D_MODEL), lambda i, *_: (layer, 0, j))


def _row_spec(width):
    return pl.BlockSpec((TT, width), lambda i, *_: (i, 0))


def _vec_spec(layer):
    return pl.BlockSpec((None, 1, D_MODEL), lambda i, *_: (layer, 0, 0))


def _mod_kernel(c_ref, w_ref, b_ref, o_ref):
    a = _silu(c_ref[...]).astype(BF16)
    o_ref[...] = _dot(a, w_ref[...].astype(BF16)) + b_ref[...]


def _adaln(c_all, ada_w, ada_b):
    tn = 1024
    n = ada_w.shape[-1]
    return pl.pallas_call(
        _mod_kernel,
        grid=(DEPTH, n // tn),
        in_specs=[
            pl.BlockSpec((N_COND, D_MODEL), lambda l, j: (0, 0)),
            pl.BlockSpec((None, D_MODEL, tn), lambda l, j: (l, 0, j)),
            pl.BlockSpec((None, 1, tn), lambda l, j: (l, 0, j)),
        ],
        out_specs=pl.BlockSpec((None, N_COND, tn), lambda l, j: (l, 0, j)),
        out_shape=jax.ShapeDtypeStruct((DEPTH, N_COND, n), F32),
        compiler_params=_params(("arbitrary", "arbitrary"), 52),
        name="adaln",
    )(c_all, ada_w, ada_b.reshape(DEPTH, 1, n))


def _norm_mod_kernel(x_ref, g_ref, sc_ref, sh_ref, h_ref):
    def body(mods):
        sc, sh = mods
        h_ref[...] = (_rms(x_ref[...], g_ref[...]) * (1 + sc) + sh).astype(h_ref.dtype)

    _by_tile_kind(pl.program_id(0), [sc_ref, sh_ref], body)


def _norm_mod(x, mod, norm_g, layer):
    return pl.pallas_call(
        _norm_mod_kernel,
        grid=(N_PT + N_ST,),
        in_specs=[_row_spec(D_MODEL), _vec_spec(layer), _tab_spec(layer, 1), _tab_spec(layer, 0)],
        out_specs=_row_spec(D_MODEL),
        out_shape=jax.ShapeDtypeStruct((N_TOK, D_MODEL), BF16),
        compiler_params=_params(("arbitrary",), 32),
        name="norm_mod",
    )(x, norm_g.reshape(DEPTH, 1, D_MODEL), mod, mod)


HALF = D_MODEL // 2
_HI16 = 0xFFFF0000


def _pack_bf16_pairs(h):
    bits = lax.bitcast_convert_type(h.astype(BF16).astype(F32), jnp.uint32)
    return lax.shift_right_logical(bits[:, :HALF], jnp.uint32(16)) | bits[:, HALF:]


def _unpack_bf16_pairs(u):
    lo = lax.bitcast_convert_type(lax.shift_left(u, jnp.uint32(16)), F32)
    hi = lax.bitcast_convert_type(u & jnp.uint32(_HI16), F32)
    return lo.astype(BF16), hi.astype(BF16)


def _route(h2, rwt, rb, comb_ref, mask_ref):
    h_hi, h_mid, _ = _split3(h2)
    w_hi, w_mid, _ = _split3(rwt)
    logits = _dot_nt(w_hi, h_hi) + _dot_nt(w_hi, h_mid) + _dot_nt(w_mid, h_hi)
    scores = jax.nn.sigmoid(logits)
    sel = scores + rb
    a = [sel[e:e + 1] for e in range(N_EXPERTS)]
    sc = [scores[e:e + 1] for e in range(N_EXPERTS)]

    best, best_g = None, None
    for g in range(N_EXPERT_GROUPS):
        a0, a1, a2, a3 = a[4 * g:4 * g + 4]
        hi01, lo01 = jnp.maximum(a0, a1), jnp.minimum(a0, a1)
        hi23, lo23 = jnp.maximum(a2, a3), jnp.minimum(a2, a3)
        gs = jnp.maximum(hi01, hi23) + jnp.maximum(jnp.minimum(hi01, hi23), jnp.maximum(lo01, lo23))
        if g == 0:
            best, best_g = gs, jnp.zeros(gs.shape, jnp.int32)
        else:
            upd = gs > best
            best = jnp.where(upd, gs, best)
            best_g = jnp.where(upd, g, best_g)

    chosen = []
    for e in range(N_EXPERTS):
        g = e // EXPERTS_PER_GROUP
        rank = jnp.zeros(a[e].shape, jnp.int32)
        for j in range(4 * g, 4 * g + 4):
            if j == e:
                continue
            ahead = (a[j] > a[e]) | ((a[j] == a[e]) & (j < e))
            rank = rank + ahead.astype(jnp.int32)
        chosen.append((best_g == g) & (rank < TOP_K))

    denom = jnp.zeros(a[0].shape, F32)
    for e in range(N_EXPERTS):
        denom = denom + jnp.where(chosen[e], sc[e], 0.0)
    for e in range(N_EXPERTS):
        comb_ref[e:e + 1, :] = jnp.where(chosen[e], sc[e] / denom, 0.0)
        mask_ref[e:e + 1, :] = jnp.where(chosen[e], 1.0, 0.0)


def _post_mix_kernel(x_ref, m_ref, g_ref, rwt_ref, rb_ref, gate_ref, sc_ref, sh_ref,
                     x1_ref, hp_ref, comb_ref, mask_ref):
    def body(mods):
        gate, sc, sh = mods
        x1 = x_ref[...] + gate * m_ref[...]
        x1_ref[...] = x1
        h2 = _rms(x1, g_ref[...]) * (1 + sc) + sh
        hp_ref[...] = _pack_bf16_pairs(h2)
        _route(h2, rwt_ref[...], rb_ref[...], comb_ref, mask_ref)

    _by_tile_kind(pl.program_id(0), [gate_ref, sc_ref, sh_ref], body)


def _post_mix(x, mixed, mod, norm_g, router_wt, router_b, layer):
    return pl.pallas_call(
        _post_mix_kernel,
        grid=(N_PT + N_ST,),
        in_specs=[_row_spec(D_MODEL), _row_spec(D_MODEL), _vec_spec(layer),
                  pl.BlockSpec((N_EXPERTS, D_MODEL), lambda i: (0, 0)),
                  pl.BlockSpec((N_EXPERTS, 1), lambda i: (0, 0)),
                  _tab_spec(layer, 2), _tab_spec(layer, 4), _tab_spec(layer, 3)],
        out_specs=[_row_spec(D_MODEL), _row_spec(HALF),
                   pl.BlockSpec((N_EXPERTS, TT), lambda i: (0, i)),
                   pl.BlockSpec((N_EXPERTS, TT), lambda i: (0, i))],
        out_shape=[jax.ShapeDtypeStruct((N_TOK, D_MODEL), F32),
                   jax.ShapeDtypeStruct((N_TOK, HALF), jnp.uint32),
                   jax.ShapeDtypeStruct((N_EXPERTS, N_TOK), F32),
                   jax.ShapeDtypeStruct((N_EXPERTS, N_TOK), F32)],
        compiler_params=_params(("arbitrary",), 40),
        name="post_mix",
    )(x, mixed, norm_g.reshape(DEPTH, 1, D_MODEL), router_wt, router_b.reshape(N_EXPERTS, 1), mod, mod, mod)


def _mm_kernel(a_ref, w_ref, o_ref, wb_ref, *, w_transposed):
    @pl.when(pl.program_id(1) == 0)
    def _():
        w = w_ref[...]
        wb_ref[...] = (w.T if w_transposed else w).astype(BF16)

    o_ref[...] = _dot(a_ref[...], wb_ref[...]).astype(o_ref.dtype)


def _matmul(a, w, layer, n_cols, name, w_transposed=False):
    m, k = a.shape
    if w_transposed:
        w_spec = pl.BlockSpec((None, MM_TN, k), lambda j, i: (layer, j, 0))
    else:
        w_spec = pl.BlockSpec((None, k, MM_TN), lambda j, i: (layer, 0, j))
    return pl.pallas_call(
        functools.partial(_mm_kernel, w_transposed=w_transposed),
        grid=(n_cols // MM_TN, m // MM_TM),
        in_specs=[pl.BlockSpec((MM_TM, k), lambda j, i: (i, 0)), w_spec],
        out_specs=pl.BlockSpec((MM_TM, MM_TN), lambda j, i: (i, j)),
        out_shape=jax.ShapeDtypeStruct((m, n_cols), F32),
        scratch_shapes=[pltpu.VMEM((k, MM_TN), BF16)],
        compiler_params=_params(("arbitrary", "arbitrary"), 48),
        name=name,
    )(a, w)


def _gate_kernel(h_ref, wg_ref, w2_ref, b_ref, o_ref):
    zg = _dot_nt(h_ref[...], wg_ref[...].astype(BF16))
    z_hi, z_mid, _ = _split3(zg)
    w_hi, w_mid, _ = _split3(w2_ref[...])
    logit = _dot(z_hi, w_hi) + _dot(z_hi, w_mid) + _dot(z_mid, w_hi) + b_ref[...]
    o_ref[...] = (jnp.minimum(logit, 0.0) - jnp.log1p(jnp.exp(-jnp.abs(logit)))) * (1.0 / GATE_NORM)


def _gate(h, w_gate, gk_w2, gk_b, layer):
    tm = 512
    return pl.pallas_call(
        _gate_kernel,
        grid=(N_TOK // tm,),
        in_specs=[
            pl.BlockSpec((tm, D_MODEL), lambda i: (i, 0)),
            pl.BlockSpec((GATE_RANK, D_MODEL), lambda i: (0, 0)),
            pl.BlockSpec((None, GATE_RANK, D_GLA_K), lambda i: (layer, 0, 0)),
            pl.BlockSpec((None, 1, D_GLA_K), lambda i: (layer, 0, 0)),
        ],
        out_specs=pl.BlockSpec((tm, D_GLA_K), lambda i: (i, 0)),
        out_shape=jax.ShapeDtypeStruct((N_TOK, D_GLA_K), F32),
        compiler_params=_params(("arbitrary",), 32),
        name="gla_gate",
    )(h, w_gate, gk_w2, gk_b.reshape(DEPTH, 1, D_GLA_K))


def _pool_windows(load, pos, pw_ref, ps_ref, o_ref):
    for g, w in enumerate(POOL_WINDOWS):
        cs = slice(g * POOL_GW, (g + 1) * POOL_GW)
        cur = load(0, cs)
        win = cur
        for k in range(1, w):
            win = win + load(k, cs)
        cnt = jnp.minimum(pos + 1, w).astype(F32)
        d = win / cnt - cur
        y = _dot(d.astype(BF16), pw_ref[g].astype(BF16))
        o_ref[:, cs] = (y * ps_ref[:, cs]).astype(o_ref.dtype)


def _pool_prompt_kernel(u_ref, pw_ref, ps_ref, o_ref, ext_ref):
    c = pl.program_id(1)

    @pl.when(c == 0)
    def _():
        ext_ref[0:POOL_HIST, :] = jnp.zeros((POOL_HIST, D_POOL), F32)

    @pl.when(c > 0)
    def _():
        ext_ref[0:POOL_HIST, :] = ext_ref[POOL_T:POOL_T + POOL_HIST, :]

    ext_ref[POOL_HIST:POOL_HIST + POOL_T, :] = u_ref[...]
    pos = c * POOL_T + lax.broadcasted_iota(jnp.int32, (POOL_T, POOL_GW), 0)

    def load(k, cs):
        return ext_ref[POOL_HIST - k:POOL_HIST - k + POOL_T, cs]

    _pool_windows(load, pos, pw_ref, ps_ref, o_ref)


def _pool_sample_kernel(u_ref, prev_ref, pw_ref, ps_ref, _mix_in, o_ref):
    def load(k, cs):
        outs = []
        for t in range(DEC_SEQ):
            p = t - k
            if p >= 0:
                outs.append(u_ref[p * DEC_BATCH:(p + 1) * DEC_BATCH, cs])
            else:
                outs.append(prev_ref[POOL_BUF + p, :, cs])
        return jnp.concatenate(outs, axis=0)

    row = lax.broadcasted_iota(jnp.int32, (N_S, POOL_GW), 0)
    pos = PAST_LEN + lax.shift_right_logical(row, DEC_BATCH.bit_length() - 1)
    _pool_windows(load, pos, pw_ref, ps_ref, o_ref)


def _pool_prompt(z, pool_w, pool_scale, layer):
    steps = SEQ // POOL_T
    return pl.pallas_call(
        _pool_prompt_kernel,
        grid=(BATCH, steps),
        in_specs=[
            pl.BlockSpec((POOL_T, D_POOL), lambda b, c: (b * steps + c, 0)),
            pl.BlockSpec((None, len(POOL_WINDOWS), POOL_GW, POOL_GW), lambda b, c: (layer, 0, 0, 0)),
            pl.BlockSpec((None, 1, D_POOL), lambda b, c: (layer, 0, 0)),
        ],
        out_specs=pl.BlockSpec((POOL_T, D_POOL), lambda b, c: (b * steps + c, 0)),
        out_shape=jax.ShapeDtypeStruct((N_TOK, D_MODEL), BF16),
        scratch_shapes=[pltpu.VMEM((POOL_HIST + POOL_T, D_POOL), F32)],
        compiler_params=_params(("arbitrary", "arbitrary"), 32),
        name="pool_prompt",
    )(z, pool_w, pool_scale.reshape(DEPTH, 1, D_POOL))


def _pool_sample(z, prev_t, pool_w, pool_scale, mix, layer):
    blk = N_P // N_S
    return pl.pallas_call(
        _pool_sample_kernel,
        grid=(1,),
        in_specs=[
            pl.BlockSpec((N_S, D_POOL), lambda i: (blk, 0)),
            pl.BlockSpec((POOL_BUF, DEC_BATCH, D_POOL), lambda i: (0, 0, 0)),
            pl.BlockSpec((None, len(POOL_WINDOWS), POOL_GW, POOL_GW), lambda i: (layer, 0, 0, 0)),
            pl.BlockSpec((None, 1, D_POOL), lambda i: (layer, 0, 0)),
            pl.BlockSpec(memory_space=pl.ANY),
        ],
        out_specs=pl.BlockSpec((N_S, D_POOL), lambda i: (blk, 0)),
        out_shape=jax.ShapeDtypeStruct((N_TOK, D_MODEL), BF16),
        input_output_aliases={4: 0},
        compiler_params=_params(("arbitrary",), 48),
        name="pool_sample",
    )(z, prev_t, pool_w, pool_scale.reshape(DEPTH, 1, D_POOL), mix)


def _gla_chunk(la, q, k, v, r, g, s_ref, nseq, seq_len):
    c = GLA_CHUNK
    mid, last = seq_len // 2, seq_len - 1

    def seq_pos(idx):
        if nseq == 1:
            return jnp.zeros_like(idx), idx
        return idx & (nseq - 1), lax.shift_right_logical(idx, nseq.bit_length() - 1)

    ri = lax.broadcasted_iota(jnp.int32, (3 * c, c), 0)
    cj = lax.broadcasted_iota(jnp.int32, (3 * c, c), 1)
    kind = lax.shift_right_logical(ri, c.bit_length() - 1)
    si, pi = seq_pos(ri & (c - 1))
    sj, pj = seq_pos(cj)
    limit = jnp.where(kind == 0, pi, jnp.where(kind == 1, mid, last))
    sums = jnp.where((si == sj) & (pj <= limit), 1.0, 0.0).astype(BF16)
    si, pi = seq_pos(lax.broadcasted_iota(jnp.int32, (c, c), 0))
    sj, pj = seq_pos(lax.broadcasted_iota(jnp.int32, (c, c), 1))
    causal = (si == sj) & (pj <= pi)

    la_hi, la_mid, _ = _split3(la)
    b3 = _dot(sums, la_hi) + _dot(sums, la_mid)
    b, b_mid, b_last = b3[0:c], b3[c:2 * c], b3[2 * c:3 * c]

    rs, _ = seq_pos(lax.broadcasted_iota(jnp.int32, (c, 128), 0))
    ls = lax.broadcasted_iota(jnp.int32, (c, 128), 1)
    pick = jnp.where(rs == (ls & (nseq - 1)), 1.0, 0.0).astype(BF16)
    blc = _dot_tn(la_hi, pick) + _dot_tn(la_mid, pick)

    qs = q * (GLA_DK ** -0.5)
    a = _dot_nt((qs * jnp.exp(b - b_mid)).astype(BF16), (k * jnp.exp(b_mid - b)).astype(BF16))
    a = jnp.where(causal, a, 0.0)
    vb = v.astype(BF16)
    o = _dot(a.astype(BF16), vb)
    q_inter = (qs * jnp.exp(b)).astype(BF16)
    k_carry = k * jnp.exp(b_last - b)
    row_seq, _ = seq_pos(lax.broadcasted_iota(jnp.int32, (c, 1), 0))
    o_inter = None
    for s in range(nseq):
        st = s_ref[s]
        oi = _dot(q_inter, st.astype(BF16))
        o_inter = oi if o_inter is None else jnp.where(row_seq == s, oi, o_inter)
        ks = k_carry if nseq == 1 else jnp.where(row_seq == s, k_carry, 0.0)
        s_ref[s] = jnp.exp(blc[:, s:s + 1]) * st + _dot_tn(ks.astype(BF16), vb)
    o = o + o_inter
    return _rms(o, g) * _silu(r)


def _gla_prompt_kernel(q_ref, k_ref, v_ref, r_ref, la_ref, g_ref, *rest):
    o_ref, s_ref = rest[-2:]
    s_ref[...] = jnp.zeros(s_ref.shape, F32)

    def chunk(ci, carry):
        rows = pl.ds(pl.multiple_of(ci * GLA_CHUNK, GLA_CHUNK), GLA_CHUNK)
        o = _gla_chunk(la_ref[rows, :], q_ref[rows, :], k_ref[rows, :], v_ref[rows, :], r_ref[rows, :],
                       g_ref[...], s_ref, 1, GLA_CHUNK)
        o_ref[rows, :] = o.astype(o_ref.dtype)
        return carry

    lax.fori_loop(0, SEQ // GLA_CHUNK, chunk, 0, unroll=8)


def _gla_sample_kernel(q_ref, k_ref, v_ref, r_ref, la_ref, g_ref, s0_ref, *rest):
    o_ref, s_ref, o_scr = rest[-3:]
    sb = pl.program_id(1)
    s_ref[...] = s0_ref[...]
    base = pl.multiple_of(sb * GLA_SEQ_BLK, GLA_SEQ_BLK)

    def rows(ref):
        return jnp.concatenate(
            [ref[pl.ds(t * DEC_BATCH + base, GLA_SEQ_BLK), :] for t in range(DEC_SEQ)], axis=0)

    o = _gla_chunk(rows(la_ref), rows(q_ref), rows(k_ref), rows(v_ref), rows(r_ref),
                   g_ref[...], s_ref, GLA_SEQ_BLK, DEC_SEQ)
    for t in range(DEC_SEQ):
        o_scr[pl.ds(t * DEC_BATCH + base, GLA_SEQ_BLK), :] = o[t * GLA_SEQ_BLK:(t + 1) * GLA_SEQ_BLK]

    @pl.when(sb == DEC_BATCH // GLA_SEQ_BLK - 1)
    def _():
        o_ref[...] = o_scr[...].astype(o_ref.dtype)


def _gla_prompt(z, log_a, norm_g, mix, s_stack, layer):
    def spec(width, off):
        return pl.BlockSpec((SEQ, width), lambda b, h: (b, off // width + h))

    ins = [z, z, z, z, log_a, norm_g.reshape(DEPTH, 1, GLA_DV), mix]
    in_specs = [spec(GLA_DK, OFF_Q), spec(GLA_DK, OFF_K), spec(GLA_DV, OFF_V), spec(GLA_DV, OFF_R),
                spec(GLA_DK, 0), pl.BlockSpec((None, 1, GLA_DV), lambda b, h: (layer, 0, 0)),
                pl.BlockSpec(memory_space=pl.ANY)]
    aliases = {6: 0}
    if s_stack is not None:
        ins.append(s_stack)
        in_specs.append(pl.BlockSpec(memory_space=pl.ANY))
        aliases[7] = 1
    return pl.pallas_call(
        _gla_prompt_kernel,
        grid=(BATCH, GLA_HEADS),
        in_specs=in_specs,
        out_specs=[
            pl.BlockSpec((SEQ, GLA_DV), lambda b, h: (b, D_POOL // GLA_DV + h)),
            pl.BlockSpec((None, 1, None, GLA_DK, GLA_DV), lambda b, h: (layer, b, h, 0, 0)),
        ],
        out_shape=[jax.ShapeDtypeStruct((N_TOK, D_MODEL), BF16),
                   jax.ShapeDtypeStruct((DEPTH, BATCH, GLA_HEADS, GLA_DK, GLA_DV), F32)],
        input_output_aliases=aliases,
        compiler_params=_params(("arbitrary", "arbitrary"), 52),
        name="gla_prompt",
    )(*ins)


def _gla_sample(z, log_a, norm_g, state_gla, mix, s_stack, layer):
    blk = N_P // N_S

    def spec(width, off):
        return pl.BlockSpec((N_S, width), lambda h, sb: (blk, off // width + h))

    ins = [z, z, z, z, log_a, norm_g.reshape(DEPTH, 1, GLA_DV), state_gla, mix]
    in_specs = [spec(GLA_DK, OFF_Q), spec(GLA_DK, OFF_K), spec(GLA_DV, OFF_V), spec(GLA_DV, OFF_R),
                spec(GLA_DK, 0), pl.BlockSpec((None, 1, GLA_DV), lambda h, sb: (layer, 0, 0)),
                pl.BlockSpec((None, GLA_SEQ_BLK, None, GLA_DK, GLA_DV), lambda h, sb: (layer, sb, h, 0, 0)),
                pl.BlockSpec(memory_space=pl.ANY)]
    aliases = {7: 0}
    if s_stack is not None:
        ins.append(s_stack)
        in_specs.append(pl.BlockSpec(memory_space=pl.ANY))
        aliases[8] = 1
    return pl.pallas_call(
        _gla_sample_kernel,
        grid=(GLA_HEADS, DEC_BATCH // GLA_SEQ_BLK),
        in_specs=in_specs,
        out_specs=[
            pl.BlockSpec((N_S, GLA_DV), lambda h, sb: (blk, D_POOL // GLA_DV + h)),
            pl.BlockSpec((None, GLA_SEQ_BLK, None, GLA_DK, GLA_DV), lambda h, sb: (layer, sb, h, 0, 0)),
        ],
        out_shape=[jax.ShapeDtypeStruct((N_TOK, D_MODEL), BF16),
                   jax.ShapeDtypeStruct((DEPTH, DEC_BATCH, GLA_HEADS, GLA_DK, GLA_DV), F32)],
        scratch_shapes=[pltpu.VMEM((N_S, GLA_DV), F32)],
        input_output_aliases=aliases,
        compiler_params=_params(("arbitrary", "arbitrary"), 52),
        name="gla_sample",
    )(*ins)


def _route_indices(mask, comb):
    mb = mask > 0.5
    cnt = jnp.sum(mb, axis=1, dtype=jnp.int32)
    tiles_e = (cnt + MOE_TM - 1) // MOE_TM
    tile_end = jnp.cumsum(tiles_e)
    tile_start = tile_end - tiles_e
    slot = tile_start[:, None] * MOE_TM + jnp.cumsum(mb, axis=1, dtype=jnp.int32) - 1
    e0 = jnp.argmax(mb, axis=0)
    e1 = N_EXPERTS - 1 - jnp.argmax(mb[::-1], axis=0)
    tok = jnp.arange(N_TOK, dtype=jnp.int32)
    slot0 = slot[e0, tok]
    slot1 = slot[e1, tok]
    w0 = comb[e0, tok][:, None]
    w1 = comb[e1, tok][:, None]
    tok_of_slot = jnp.zeros((N_SLOTS,), jnp.int32).at[slot0].set(tok).at[slot1].set(tok)
    tile = jnp.arange(N_SLOT_TILES, dtype=jnp.int32)
    tile_expert = jnp.minimum(jnp.searchsorted(tile_end, tile, side="right"), N_EXPERTS - 1).astype(jnp.int32)
    tile_valid = (tile < tile_end[-1]).astype(jnp.int32)
    prev = jnp.concatenate([jnp.full((1,), -1, jnp.int32), tile_expert[:-1]])
    tile_new = (tile_expert != prev).astype(jnp.int32)
    return tok_of_slot, tile_expert, tile_valid, tile_new, slot0, slot1, w0, w1


def _gather_kernel(tok_ref, valid_ref, h_hbm, o_ref, buf, sem):
    t = pl.program_id(0)

    def start_tile(tile):
        slot = tile % 2
        base = tile * MOE_TM

        def issue(r, carry):
            tok = tok_ref[base + r]
            pltpu.make_async_copy(h_hbm.at[pl.ds(tok, 1)], buf.at[slot, pl.ds(r, 1)], sem.at[slot]).start()
            return carry

        lax.fori_loop(0, MOE_TM, issue, 0, unroll=8)

    @pl.when(t == 0)
    def _():
        start_tile(t)

    @pl.when(t + 1 < N_SLOT_TILES)
    def _():
        @pl.when(valid_ref[t + 1] != 0)
        def _():
            start_tile(t + 1)

    @pl.when(valid_ref[t] != 0)
    def _():
        slot = t % 2
        pltpu.make_async_copy(h_hbm.at[pl.ds(0, MOE_TM)], buf.at[slot], sem.at[slot]).wait()
        lo, hi = _unpack_bf16_pairs(buf[slot])
        o_ref[:, :HALF] = lo
        o_ref[:, HALF:] = hi

    @pl.when(valid_ref[t] == 0)
    def _():
        o_ref[...] = jnp.zeros(o_ref.shape, o_ref.dtype)


def _moe_gather(h2_packed, tok_of_slot, tile_valid):
    return pl.pallas_call(
        _gather_kernel,
        grid_spec=pltpu.PrefetchScalarGridSpec(
            num_scalar_prefetch=2,
            grid=(N_SLOT_TILES,),
            in_specs=[pl.BlockSpec(memory_space=pl.ANY)],
            out_specs=pl.BlockSpec((MOE_TM, D_MODEL), lambda t, *_: (t, 0)),
            scratch_shapes=[pltpu.VMEM((2, MOE_TM, HALF), jnp.uint32), pltpu.SemaphoreType.DMA((2,))],
        ),
        out_shape=jax.ShapeDtypeStruct((N_SLOTS, D_MODEL), BF16),
        compiler_params=_params(("arbitrary",), 32),
        name="moe_gather",
    )(tok_of_slot, tile_valid, h2_packed)


def _moe_up_kernel(te_ref, tv_ref, tn_ref, x_ref, w1_ref, w3_ref, o_ref, w1b, w3b):
    t = pl.program_id(1)

    @pl.when(tv_ref[t] != 0)
    def _():
        @pl.when(tn_ref[t] != 0)
        def _():
            w1b[...] = w1_ref[...].astype(BF16)
            w3b[...] = w3_ref[...].astype(BF16)

        x = x_ref[...]
        o_ref[...] = (_silu(_dot(x, w1b[...])) * _dot(x, w3b[...])).astype(o_ref.dtype)

    @pl.when(tv_ref[t] == 0)
    def _():
        o_ref[...] = jnp.zeros(o_ref.shape, o_ref.dtype)


def _moe_up(xs, w1, w3, tile_expert, tile_valid, tile_new, layer):
    def wspec():
        return pl.BlockSpec((None, None, D_MODEL, MOE_TF), lambda f, t, te, tv, tn: (layer, te[t], 0, f))

    return pl.pallas_call(
        _moe_up_kernel,
        grid_spec=pltpu.PrefetchScalarGridSpec(
            num_scalar_prefetch=3,
            grid=(D_EXPERT // MOE_TF, N_SLOT_TILES),
            in_specs=[pl.BlockSpec((MOE_TM, D_MODEL), lambda f, t, *_: (t, 0)), wspec(), wspec()],
            out_specs=pl.BlockSpec((MOE_TM, MOE_TF), lambda f, t, *_: (t, f)),
            scratch_shapes=[pltpu.VMEM((D_MODEL, MOE_TF), BF16)] * 2,
        ),
        out_shape=jax.ShapeDtypeStruct((N_SLOTS, D_EXPERT), BF16),
        compiler_params=_params(("arbitrary", "arbitrary"), 54),
        name="moe_up",
    )(tile_expert, tile_valid, tile_new, xs, w1, w3)


def _moe_down_kernel(te_ref, tv_ref, tn_ref, h_ref, w2_ref, o_ref, w2b):
    t = pl.program_id(1)

    @pl.when(tv_ref[t] != 0)
    def _():
        @pl.when(tn_ref[t] != 0)
        def _():
            w2b[...] = w2_ref[...].astype(BF16)

        o_ref[...] = _dot(h_ref[...], w2b[...])

    @pl.when(tv_ref[t] == 0)
    def _():
        o_ref[...] = jnp.zeros(o_ref.shape, o_ref.dtype)


def _moe_down(hid, w2, tile_expert, tile_valid, tile_new, layer):
    return pl.pallas_call(
        _moe_down_kernel,
        grid_spec=pltpu.PrefetchScalarGridSpec(
            num_scalar_prefetch=3,
            grid=(D_MODEL // MOE_TN, N_SLOT_TILES),
            in_specs=[
                pl.BlockSpec((MOE_TM, D_EXPERT), lambda n, t, *_: (t, 0)),
                pl.BlockSpec((None, None, D_EXPERT, MOE_TN), lambda n, t, te, tv, tn: (layer, te[t], 0, n)),
            ],
            out_specs=pl.BlockSpec((MOE_TM, MOE_TN), lambda n, t, *_: (t, n)),
            scratch_shapes=[pltpu.VMEM((D_EXPERT, MOE_TN), BF16)],
        ),
        out_shape=jax.ShapeDtypeStruct((N_SLOTS, D_MODEL), F32),
        compiler_params=_params(("arbitrary", "arbitrary"), 56),
        name="moe_down",
    )(tile_expert, tile_valid, tile_new, hid, w2)


def _combine_kernel(s0_ref, s1_ref, y_hbm, x_ref, w0_ref, w1_ref, gate_ref, g_ref, *rest,
                    tile0, n_tiles, final):
    if final:
        out_ref, bufs, sems = rest
        tabs = [gate_ref]
    else:
        sc_ref, sh_ref, x2_ref, h_ref, bufs, sems = rest
        tabs = [gate_ref, sc_ref, sh_ref]
    step = pl.program_id(0)
    i = step + tile0

    def start_tile(j):
        par = j % 2
        base = (j + tile0) * TT

        def issue(r, carry):
            for k, s_ref in enumerate((s0_ref, s1_ref)):
                pltpu.make_async_copy(y_hbm.at[pl.ds(s_ref[base + r], 1)], bufs.at[par, k, pl.ds(r, 1)],
                                      sems.at[par, k]).start()
            return carry

        lax.fori_loop(0, TT, issue, 0, unroll=8)

    @pl.when(step == 0)
    def _():
        start_tile(step)

    @pl.when(step + 1 < n_tiles)
    def _():
        start_tile(step + 1)

    par = step % 2
    for k in range(TOP_K):
        pltpu.make_async_copy(y_hbm.at[pl.ds(0, TT)], bufs.at[par, k], sems.at[par, k]).wait()

    def body(mods):
        moe = w0_ref[...] * bufs[par, 0] + w1_ref[...] * bufs[par, 1]
        x2 = x_ref[...] + mods[0] * moe
        if final:
            out_ref[...] = _rms(x2, g_ref[...])
        else:
            x2_ref[...] = x2
            h_ref[...] = (_rms(x2, g_ref[...]) * (1 + mods[1]) + mods[2]).astype(h_ref.dtype)

    _by_tile_kind(i, tabs, body)


def _combine(y, x1, slot0, slot1, w0, w1, mod, layer, norm_g, *, final, tile0=0, n_tiles=N_PT + N_ST):
    def rows(width):
        return pl.BlockSpec((TT, width), lambda i, *_: (i + tile0, 0))

    in_specs = [pl.BlockSpec(memory_space=pl.ANY), rows(D_MODEL), rows(1), rows(1), _tab_spec(layer, 5)]
    ins = [y, x1, w0, w1, mod]
    if final:
        in_specs.append(pl.BlockSpec((1, D_MODEL), lambda i, *_: (0, 0)))
        ins.append(norm_g.reshape(1, D_MODEL))
        out_specs = _row_spec(D_MODEL)
        out_shape = jax.ShapeDtypeStruct((n_tiles * TT, D_MODEL), F32)
    else:
        in_specs += [_vec_spec(layer + 1), _tab_spec(layer + 1, 1), _tab_spec(layer + 1, 0)]
        ins += [norm_g.reshape(DEPTH, 1, D_MODEL), mod, mod]
        out_specs = [_row_spec(D_MODEL), _row_spec(D_MODEL)]
        out_shape = [jax.ShapeDtypeStruct((N_TOK, D_MODEL), F32), jax.ShapeDtypeStruct((N_TOK, D_MODEL), BF16)]
    return pl.pallas_call(
        functools.partial(_combine_kernel, tile0=tile0, n_tiles=n_tiles, final=final),
        grid_spec=pltpu.PrefetchScalarGridSpec(
            num_scalar_prefetch=2,
            grid=(n_tiles,),
            in_specs=in_specs,
            out_specs=out_specs,
            scratch_shapes=[pltpu.VMEM((2, TOP_K, TT, D_MODEL), F32), pltpu.SemaphoreType.DMA((2, TOP_K))],
        ),
        out_shape=out_shape,
        compiler_params=_params(("arbitrary",), 40),
        name="moe_combine_final" if final else "moe_combine",
    )(slot0, slot1, *ins)


def kernel(x_prompt, x_sample, c_prompt, c_sample, state_pool, state_gla, ada_w, ada_b, norm1_g, norm2_g, w_in, pool_w, pool_scale, gk_w2, gk_b, gla_norm_g, w_out, router_w, router_b, moe_w1, moe_w3, moe_w2, final_g):
    x = jnp.concatenate([x_prompt.reshape(N_P, D_MODEL),
                         x_sample.transpose(1, 0, 2).reshape(N_S, D_MODEL)], axis=0)
    mod = _adaln(jnp.concatenate([c_sample, c_prompt], axis=0), ada_w, ada_b)
    router_wt = router_w.T
    w_in_t = jnp.swapaxes(w_in, 1, 2)
    h = _norm_mod(x, mod, norm1_g, 0)

    pool_p, pool_s = [], []
    gla_p = gla_s = None
    for l in range(DEPTH):
        z = _matmul(h, w_in_t, l, OFF_G, "mm_in", w_transposed=True)
        log_a = _gate(h, w_in_t[l, OFF_G:, :], gk_w2, gk_b, l)

        prev_t = state_pool[l].transpose(1, 0, 2)
        mix = _pool_prompt(z, pool_w, pool_scale, l)
        mix = _pool_sample(z, prev_t, pool_w, pool_scale, mix, l)
        mix, gla_p = _gla_prompt(z, log_a, gla_norm_g, mix, gla_p, l)
        mix, gla_s = _gla_sample(z, log_a, gla_norm_g, state_gla, mix, gla_s, l)
        pool_p.append(jnp.stack([z[b * SEQ + SEQ - POOL_BUF:(b + 1) * SEQ, :D_POOL] for b in range(BATCH)]))
        u_s = z[N_P:, :D_POOL].reshape(DEC_SEQ, DEC_BATCH, D_POOL)
        pool_s.append(jnp.concatenate([prev_t[DEC_SEQ:], u_s], axis=0).transpose(1, 0, 2))

        mixed = _matmul(mix, w_out, l, D_MODEL, "mm_out")
        x1, h2, comb, mask = _post_mix(x, mixed, mod, norm2_g, router_wt, router_b, l)
        tok_of_slot, tile_expert, tile_valid, tile_new, slot0, slot1, w0, w1 = _route_indices(mask, comb)
        xs = _moe_gather(h2, tok_of_slot, tile_valid)
        hid = _moe_up(xs, moe_w1, moe_w3, tile_expert, tile_valid, tile_new, l)
        y = _moe_down(hid, moe_w2, tile_expert, tile_valid, tile_new, l)
        if l + 1 < DEPTH:
            x, h = _combine(y, x1, slot0, slot1, w0, w1, mod, l, norm1_g, final=False)
        else:
            y_p = _combine(y, x1, slot0, slot1, w0, w1, mod, l, final_g, final=True, tile0=0, n_tiles=N_PT)
            y_s = _combine(y, x1, slot0, slot1, w0, w1, mod, l, final_g, final=True, tile0=N_PT, n_tiles=N_ST)

    return (y_p.reshape(BATCH, SEQ, D_MODEL),
            y_s.reshape(DEC_SEQ, DEC_BATCH, D_MODEL).transpose(1, 0, 2),
            jnp.stack(pool_p), gla_p, jnp.stack(pool_s), gla_s)
```

```python
import functools

import jax
import jax.numpy as jnp
from jax import lax
from jax.experimental import pallas as pl
from jax.experimental.pallas import tpu as pltpu

F32 = jnp.float32
BF16 = jnp.bfloat16

D_MODEL = 4096
BATCH = 4
SEQ = 2048
DEPTH = 4
DEC_BATCH = 128
DEC_SEQ = 8
PAST_LEN = 16384
D_POOL = 1024
POOL_WINDOWS = (2, 4, 8, 16)
POOL_GW = 256
POOL_BUF = 15
D_GLA_V = 3072
GLA_DV = 512
GLA_HEADS = 6
GLA_DK = 256
D_GLA_K = 1536
GATE_RANK = 16
GATE_NORM = 16.0
GLA_CHUNK = 64
OFF_Q = D_POOL
OFF_K = OFF_Q + D_GLA_K
OFF_V = OFF_K + D_GLA_K
OFF_R = OFF_V + D_GLA_V
OFF_G = OFF_R + D_GLA_V
N_EXPERTS = 16
N_EXPERT_GROUPS = 4
EXPERTS_PER_GROUP = 4
TOP_K = 2
D_EXPERT = 1024
N_MOD = 6
EPS = 1e-6

N_P = BATCH * SEQ
N_S = DEC_BATCH * DEC_SEQ
N_TOK = N_P + N_S
N_COND = DEC_BATCH + BATCH

TT = 128
N_PT = N_P // TT
N_ST = N_S // TT
TILES_PER_SEQ = SEQ // TT
MM_TM = 1024
MM_TN = 512
POOL_T = 256
POOL_HIST = 16
MOE_TM = 256
MOE_TF = 512
N_SLOT_TILES = (TOP_K * N_TOK) // MOE_TM + N_EXPERTS
N_SLOTS = N_SLOT_TILES * MOE_TM
GLA_SEQ_BLK = 8

VMEM_MIB = 1024 * 1024


def _params(semantics, vmem_mib):
    return pltpu.CompilerParams(dimension_semantics=semantics, vmem_limit_bytes=vmem_mib * VMEM_MIB)


def _split3(x):
    hi = x.astype(BF16)
    r = x - hi.astype(F32)
    mid = r.astype(BF16)
    lo = (r - mid.astype(F32)).astype(BF16)
    return hi, mid, lo


def _dot(a, b):
    return jnp.dot(a, b, preferred_element_type=F32)


def _dot_nt(a, b):
    return lax.dot_general(a, b, (((1,), (1,)), ((), ())), preferred_element_type=F32)


def _dot_tn(a, b):
    return lax.dot_general(a, b, (((0,), (0,)), ((), ())), preferred_element_type=F32)


def _rms(x, g):
    return x * lax.rsqrt(jnp.mean(x * x, axis=-1, keepdims=True) + EPS) * g


def _silu(x):
    return x * jax.nn.sigmoid(x)


def _by_tile_kind(i, tabs, body):
    @pl.when(i < N_PT)
    def _():
        row = DEC_BATCH + i // TILES_PER_SEQ
        body([t[pl.ds(row, 1), :] for t in tabs])

    @pl.when(i >= N_PT)
    def _():
        body([t[0:DEC_BATCH, :] for t in tabs])


def _tab_spec(layer, j):
    return pl.BlockSpec((None, N_COND, D_MODEL), lambda i, *_: (layer, 0, j))


def _row_spec(width):
    return pl.BlockSpec((TT, width), lambda i, *_: (i, 0))


def _vec_spec(layer):
    return pl.BlockSpec((None, 1, D_MODEL), lambda i, *_: (layer, 0, 0))


def _mod_kernel(c_ref, w_ref, b_ref, o_ref):
    a = _silu(c_ref[...]).astype(BF16)
    o_ref[...] = _dot(a, w_ref[...].astype(BF16)) + b_ref[...]


def _adaln(c_all, ada_w, ada_b):
    tn = 1024
    n = ada_w.shape[-1]
    return pl.pallas_call(
        _mod_kernel,
        grid=(DEPTH, n // tn),
        in_specs=[
            pl.BlockSpec((N_COND, D_MODEL), lambda l, j: (0, 0)),
            pl.BlockSpec((None, D_MODEL, tn), lambda l, j: (l, 0, j)),
            pl.BlockSpec((None, 1, tn), lambda l, j: (l, 0, j)),
        ],
        out_specs=pl.BlockSpec((None, N_COND, tn), lambda l, j: (l, 0, j)),
        out_shape=jax.ShapeDtypeStruct((DEPTH, N_COND, n), F32),
        compiler_params=_params(("arbitrary", "arbitrary"), 52),
        name="adaln",
    )(c_all, ada_w, ada_b.reshape(DEPTH, 1, n))


def _norm_mod_kernel(x_ref, g_ref, sc_ref, sh_ref, h_ref):
    def body(mods):
        sc, sh = mods
        h_ref[...] = (_rms(x_ref[...], g_ref[...]) * (1 + sc) + sh).astype(h_ref.dtype)

    _by_tile_kind(pl.program_id(0), [sc_ref, sh_ref], body)


def _norm_mod(x, mod, norm_g, layer):
    return pl.pallas_call(
        _norm_mod_kernel,
        grid=(N_PT + N_ST,),
        in_specs=[_row_spec(D_MODEL), _vec_spec(layer), _tab_spec(layer, 1), _tab_spec(layer, 0)],
        out_specs=_row_spec(D_MODEL),
        out_shape=jax.ShapeDtypeStruct((N_TOK, D_MODEL), BF16),
        compiler_params=_params(("arbitrary",), 32),
        name="norm_mod",
    )(x, norm_g.reshape(DEPTH, 1, D_MODEL), mod, mod)


HALF = D_MODEL // 2
_HI16 = 0xFFFF0000


def _pack_bf16_pairs(h):
    bits = lax.bitcast_convert_type(h.astype(BF16).astype(F32), jnp.uint32)
    return lax.shift_right_logical(bits[:, :HALF], jnp.uint32(16)) | bits[:, HALF:]


def _unpack_bf16_pairs(u):
    lo = lax.bitcast_convert_type(lax.shift_left(u, jnp.uint32(16)), F32)
    hi = lax.bitcast_convert_type(u & jnp.uint32(_HI16), F32)
    return lo.astype(BF16), hi.astype(BF16)


def _route(h2, rwt, rb, comb_ref, mask_ref):
    h_hi, h_mid, _ = _split3(h2)
    w_hi, w_mid, _ = _split3(rwt)
    logits = _dot_nt(w_hi, h_hi) + _dot_nt(w_hi, h_mid) + _dot_nt(w_mid, h_hi)
    scores = jax.nn.sigmoid(logits)
    sel = scores + rb
    a = [sel[e:e + 1] for e in range(N_EXPERTS)]
    sc = [scores[e:e + 1] for e in range(N_EXPERTS)]

    best, best_g = None, None
    for g in range(N_EXPERT_GROUPS):
        a0, a1, a2, a3 = a[4 * g:4 * g + 4]
        hi01, lo01 = jnp.maximum(a0, a1), jnp.minimum(a0, a1)
        hi23, lo23 = jnp.maximum(a2, a3), jnp.minimum(a2, a3)
        gs = jnp.maximum(hi01, hi23) + jnp.maximum(jnp.minimum(hi01, hi23), jnp.maximum(lo01, lo23))
        if g == 0:
            best, best_g = gs, jnp.zeros(gs.shape, jnp.int32)
        else:
            upd = gs > best
            best = jnp.where(upd, gs, best)
            best_g = jnp.where(upd, g, best_g)

    chosen = []
    for e in range(N_EXPERTS):
        g = e // EXPERTS_PER_GROUP
        rank = jnp.zeros(a[e].shape, jnp.int32)
        for j in range(4 * g, 4 * g + 4):
            if j == e:
                continue
            ahead = (a[j] > a[e]) | ((a[j] == a[e]) & (j < e))
            rank = rank + ahead.astype(jnp.int32)
        chosen.append((best_g == g) & (rank < TOP_K))

    denom = jnp.zeros(a[0].shape, F32)
    for e in range(N_EXPERTS):
        denom = denom + jnp.where(chosen[e], sc[e], 0.0)
    for e in range(N_EXPERTS):
        comb_ref[e:e + 1, :] = jnp.where(chosen[e], sc[e] / denom, 0.0)
        mask_ref[e:e + 1, :] = jnp.where(chosen[e], 1.0, 0.0)


def _post_mix_kernel(x_ref, m_ref, g_ref, rwt_ref, rb_ref, gate_ref, sc_ref, sh_ref,
                     x1_ref, hp_ref, comb_ref, mask_ref):
    def body(mods):
        gate, sc, sh = mods
        x1 = x_ref[...] + gate * m_ref[...]
        x1_ref[...] = x1
        h2 = _rms(x1, g_ref[...]) * (1 + sc) + sh
        hp_ref[...] = _pack_bf16_pairs(h2)
        _route(h2, rwt_ref[...], rb_ref[...], comb_ref, mask_ref)

    _by_tile_kind(pl.program_id(0), [gate_ref, sc_ref, sh_ref], body)


def _post_mix(x, mixed, mod, norm_g, router_wt, router_b, layer):
    return pl.pallas_call(
        _post_mix_kernel,
        grid=(N_PT + N_ST,),
        in_specs=[_row_spec(D_MODEL), _row_spec(D_MODEL), _vec_spec(layer),
                  pl.BlockSpec((N_EXPERTS, D_MODEL), lambda i: (0, 0)),
                  pl.BlockSpec((N_EXPERTS, 1), lambda i: (0, 0)),
                  _tab_spec(layer, 2), _tab_spec(layer, 4), _tab_spec(layer, 3)],
        out_specs=[_row_spec(D_MODEL), _row_spec(HALF),
                   pl.BlockSpec((N_EXPERTS, TT), lambda i: (0, i)),
                   pl.BlockSpec((N_EXPERTS, TT), lambda i: (0, i))],
        out_shape=[jax.ShapeDtypeStruct((N_TOK, D_MODEL), F32),
                   jax.ShapeDtypeStruct((N_TOK, HALF), jnp.uint32),
                   jax.ShapeDtypeStruct((N_EXPERTS, N_TOK), F32),
                   jax.ShapeDtypeStruct((N_EXPERTS, N_TOK), F32)],
        compiler_params=_params(("arbitrary",), 40),
        name="post_mix",
    )(x, mixed, norm_g.reshape(DEPTH, 1, D_MODEL), router_wt, router_b.reshape(N_EXPERTS, 1), mod, mod, mod)


def _mm_kernel(a_ref, w_ref, o_ref, wb_ref, *, w_transposed):
    @pl.when(pl.program_id(1) == 0)
    def _():
        w = w_ref[...]
        wb_ref[...] = (w.T if w_transposed else w).astype(BF16)

    o_ref[...] = _dot(a_ref[...], wb_ref[...]).astype(o_ref.dtype)


def _matmul(a, w, layer, n_cols, name, w_transposed=False):
    m, k = a.shape
    if w_transposed:
        w_spec = pl.BlockSpec((None, MM_TN, k), lambda j, i: (layer, j, 0))
    else:
        w_spec = pl.BlockSpec((None, k, MM_TN), lambda j, i: (layer, 0, j))
    return pl.pallas_call(
        functools.partial(_mm_kernel, w_transposed=w_transposed),
        grid=(n_cols // MM_TN, m // MM_TM),
        in_specs=[pl.BlockSpec((MM_TM, k), lambda j, i: (i, 0)), w_spec],
        out_specs=pl.BlockSpec((MM_TM, MM_TN), lambda j, i: (i, j)),
        out_shape=jax.ShapeDtypeStruct((m, n_cols), F32),
        scratch_shapes=[pltpu.VMEM((k, MM_TN), BF16)],
        compiler_params=_params(("arbitrary", "arbitrary"), 48),
        name=name,
    )(a, w)


def _gate_kernel(h_ref, wg_ref, w2_ref, b_ref, o_ref):
    wg = wg_ref[...].astype(BF16)
    z3 = _dot_nt(h_ref[...], jnp.concatenate([wg, wg, wg], axis=0))
    z_hi = z3.astype(BF16).astype(F32)
    lane = lax.broadcasted_iota(jnp.int32, z3.shape, 1)
    lhs = jnp.where(lane < 2 * GATE_RANK, z_hi, z3 - z_hi).astype(BF16)
    w_hi, w_mid, _ = _split3(w2_ref[...])
    logit = _dot(lhs, jnp.concatenate([w_hi, w_mid, w_hi], axis=0)) + b_ref[...]
    o_ref[...] = (jnp.minimum(logit, 0.0) - jnp.log(1.0 + jnp.exp(-jnp.abs(logit)))) * (1.0 / GATE_NORM)


def _gate(h, w_gate, gk_w2, gk_b, layer):
    tm = 512
    return pl.pallas_call(
        _gate_kernel,
        grid=(N_TOK // tm,),
        in_specs=[
            pl.BlockSpec((tm, D_MODEL), lambda i: (i, 0)),
            pl.BlockSpec((GATE_RANK, D_MODEL), lambda i: (0, 0)),
            pl.BlockSpec((None, GATE_RANK, D_GLA_K), lambda i: (layer, 0, 0)),
            pl.BlockSpec((None, 1, D_GLA_K), lambda i: (layer, 0, 0)),
        ],
        out_specs=pl.BlockSpec((tm, D_GLA_K), lambda i: (i, 0)),
        out_shape=jax.ShapeDtypeStruct((N_TOK, D_GLA_K), F32),
        compiler_params=_params(("arbitrary",), 32),
        name="gla_gate",
    )(h, w_gate, gk_w2, gk_b.reshape(DEPTH, 1, D_GLA_K))


def _pool_windows(load, pos, pw_ref, ps_ref, o_ref):
    for g, w in enumerate(POOL_WINDOWS):
        cs = slice(g * POOL_GW, (g + 1) * POOL_GW)
        cur = load(0, cs)
        win = cur
        for k in range(1, w):
            win = win + load(k, cs)
        cnt = jnp.minimum(pos + 1, w).astype(F32)
        d = win / cnt - cur
        y = _dot(d.astype(BF16), pw_ref[g].astype(BF16))
        o_ref[:, cs] = (y * ps_ref[:, cs]).astype(o_ref.dtype)


def _pool_prompt_kernel(u_ref, pw_ref, ps_ref, o_ref, ext_ref):
    c = pl.program_id(1)

    @pl.when(c == 0)
    def _():
        ext_ref[0:POOL_HIST, :] = jnp.zeros((POOL_HIST, D_POOL), F32)

    @pl.when(c > 0)
    def _():
        ext_ref[0:POOL_HIST, :] = ext_ref[POOL_T:POOL_T + POOL_HIST, :]

    ext_ref[POOL_HIST:POOL_HIST + POOL_T, :] = u_ref[...]
    pos = c * POOL_T + lax.broadcasted_iota(jnp.int32, (POOL_T, POOL_GW), 0)

    def load(k, cs):
        return ext_ref[POOL_HIST - k:POOL_HIST - k + POOL_T, cs]

    _pool_windows(load, pos, pw_ref, ps_ref, o_ref)


def _pool_sample_kernel(u_ref, prev_ref, pw_ref, ps_ref, _mix_in, o_ref):
    def load(k, cs):
        outs = []
        for t in range(DEC_SEQ):
            p = t - k
            if p >= 0:
                outs.append(u_ref[p * DEC_BATCH:(p + 1) * DEC_BATCH, cs])
            else:
                outs.append(prev_ref[POOL_BUF + p, :, cs])
        return jnp.concatenate(outs, axis=0)

    row = lax.broadcasted_iota(jnp.int32, (N_S, POOL_GW), 0)
    pos = PAST_LEN + lax.shift_right_logical(row, DEC_BATCH.bit_length() - 1)
    _pool_windows(load, pos, pw_ref, ps_ref, o_ref)


def _pool_prompt(z, pool_w, pool_scale, layer):
    steps = SEQ // POOL_T
    return pl.pallas_call(
        _pool_prompt_kernel,
        grid=(BATCH, steps),
        in_specs=[
            pl.BlockSpec((POOL_T, D_POOL), lambda b, c: (b * steps + c, 0)),
            pl.BlockSpec((None, len(POOL_WINDOWS), POOL_GW, POOL_GW), lambda b, c: (layer, 0, 0, 0)),
            pl.BlockSpec((None, 1, D_POOL), lambda b, c: (layer, 0, 0)),
        ],
        out_specs=pl.BlockSpec((POOL_T, D_POOL), lambda b, c: (b * steps + c, 0)),
        out_shape=jax.ShapeDtypeStruct((N_TOK, D_MODEL), BF16),
        scratch_shapes=[pltpu.VMEM((POOL_HIST + POOL_T, D_POOL), F32)],
        compiler_params=_params(("arbitrary", "arbitrary"), 32),
        name="pool_prompt",
    )(z, pool_w, pool_scale.reshape(DEPTH, 1, D_POOL))


def _pool_sample(z, prev_t, pool_w, pool_scale, mix, layer):
    blk = N_P // N_S
    return pl.pallas_call(
        _pool_sample_kernel,
        grid=(1,),
        in_specs=[
            pl.BlockSpec((N_S, D_POOL), lambda i: (blk, 0)),
            pl.BlockSpec((POOL_BUF, DEC_BATCH, D_POOL), lambda i: (0, 0, 0)),
            pl.BlockSpec((None, len(POOL_WINDOWS), POOL_GW, POOL_GW), lambda i: (layer, 0, 0, 0)),
            pl.BlockSpec((None, 1, D_POOL), lambda i: (layer, 0, 0)),
            pl.BlockSpec(memory_space=pl.ANY),
        ],
        out_specs=pl.BlockSpec((N_S, D_POOL), lambda i: (blk, 0)),
        out_shape=jax.ShapeDtypeStruct((N_TOK, D_MODEL), BF16),
        input_output_aliases={4: 0},
        compiler_params=_params(("arbitrary",), 48),
        name="pool_sample",
    )(z, prev_t, pool_w, pool_scale.reshape(DEPTH, 1, D_POOL), mix)


def _gla_chunk(la, q, k, v, r, g, s_ref, nseq, n_sub, seq_len):
    c = GLA_CHUNK
    ct = c * n_sub
    mid = seq_len // 2
    assert n_sub * nseq <= 8 and (n_sub == 1 or nseq == 1)

    def group_pos(idx):
        sub = lax.shift_right_logical(idx, c.bit_length() - 1)
        within = idx & (c - 1)
        return sub * nseq + (within & (nseq - 1)), lax.shift_right_logical(within, nseq.bit_length() - 1)

    ri = lax.broadcasted_iota(jnp.int32, (ct + 16, ct), 0)
    gj, pj = group_pos(lax.broadcasted_iota(jnp.int32, (ct + 16, ct), 1))
    gi, pi = group_pos(ri)
    extra = ri - ct
    in_sum = (((ri < ct) & (gi == gj) & (pj <= pi))
              | ((ri >= ct) & ((extra & 7) == gj) & ((extra >= 8) | (pj <= mid))))
    sums = jnp.where(in_sum, 1.0, 0.0).astype(BF16)

    la_hi, la_mid, _ = _split3(la)
    b3 = _dot(sums, la_hi) + _dot(sums, la_mid)
    b, mids, tots = b3[0:ct], b3[ct:ct + 8], b3[ct + 8:ct + 16]
    tots_col = tots.T

    def per_row(t8):
        if nseq == 1:
            return jnp.concatenate([jnp.broadcast_to(t8[s:s + 1], (c, GLA_DK)) for s in range(n_sub)], axis=0)
        return jnp.concatenate([t8] * (c // 8), axis=0)

    b_mid, b_last = per_row(mids), per_row(tots)
    gi, pi = group_pos(lax.broadcasted_iota(jnp.int32, (ct, ct), 0))
    gj, pj = group_pos(lax.broadcasted_iota(jnp.int32, (ct, ct), 1))
    row_g, _ = group_pos(lax.broadcasted_iota(jnp.int32, (ct, 1), 0))

    qs = q * (GLA_DK ** -0.5)
    q_in = qs * jnp.exp(b)
    k_out = k * jnp.exp(b_last - b)
    a = _dot_nt((qs * jnp.exp(b - b_mid)).astype(BF16), (k * jnp.exp(b_mid - b)).astype(BF16))
    a = jnp.where((gi == gj) & (pj <= pi), a, 0.0)
    if n_sub == 2:
        a = jnp.where(gi > gj, _dot_nt(q_in.astype(BF16), k_out.astype(BF16)), a)
    vb = v.astype(BF16)
    o = _dot(a.astype(BF16), vb)
    if n_sub == 2:
        st = s_ref[0]
        q_s = jnp.where(row_g == 1, q_in * jnp.exp(tots[0:1]), q_in)
        k_s = jnp.where(row_g == 0, k_out * jnp.exp(tots[1:2]), k_out)
        o = o + _dot(q_s.astype(BF16), st.astype(BF16))
        s_ref[0] = jnp.exp(tots_col[:, 0:1] + tots_col[:, 1:2]) * st + _dot_tn(k_s.astype(BF16), vb)
    else:
        q_inter = q_in.astype(BF16)
        o_inter = None
        for s in range(nseq):
            st = s_ref[s]
            oi = _dot(q_inter, st.astype(BF16))
            o_inter = oi if o_inter is None else jnp.where(row_g == s, oi, o_inter)
            ks = k_out if nseq == 1 else jnp.where(row_g == s, k_out, 0.0)
            s_ref[s] = jnp.exp(tots_col[:, s:s + 1]) * st + _dot_tn(ks.astype(BF16), vb)
        o = o + o_inter
    return _rms(o, g) * _silu(r)


GLA_SUB = 2


def _gla_prompt_kernel(q_ref, k_ref, v_ref, r_ref, la_ref, g_ref, *rest):
    o_ref, s_ref = rest[-2:]
    s_ref[...] = jnp.zeros(s_ref.shape, F32)
    n_rows = GLA_SUB * GLA_CHUNK

    def chunk(ci, carry):
        rows = pl.ds(pl.multiple_of(ci * n_rows, n_rows), n_rows)
        o = _gla_chunk(la_ref[rows, :], q_ref[rows, :], k_ref[rows, :], v_ref[rows, :], r_ref[rows, :],
                       g_ref[...], s_ref, 1, GLA_SUB, GLA_CHUNK)
        o_ref[rows, :] = o.astype(o_ref.dtype)
        return carry

    lax.fori_loop(0, SEQ // n_rows, chunk, 0, unroll=4)


def _gla_sample_kernel(q_ref, k_ref, v_ref, r_ref, la_ref, g_ref, s0_ref, *rest):
    o_ref, s_ref, o_scr = rest[-3:]
    sb = pl.program_id(1)
    s_ref[...] = s0_ref[...]
    base = pl.multiple_of(sb * GLA_SEQ_BLK, GLA_SEQ_BLK)

    def rows(ref):
        return jnp.concatenate(
            [ref[pl.ds(t * DEC_BATCH + base, GLA_SEQ_BLK), :] for t in range(DEC_SEQ)], axis=0)

    o = _gla_chunk(rows(la_ref), rows(q_ref), rows(k_ref), rows(v_ref), rows(r_ref),
                   g_ref[...], s_ref, GLA_SEQ_BLK, 1, DEC_SEQ)
    for t in range(DEC_SEQ):
        o_scr[pl.ds(t * DEC_BATCH + base, GLA_SEQ_BLK), :] = o[t * GLA_SEQ_BLK:(t + 1) * GLA_SEQ_BLK]

    @pl.when(sb == DEC_BATCH // GLA_SEQ_BLK - 1)
    def _():
        o_ref[...] = o_scr[...].astype(o_ref.dtype)


def _gla_prompt(z, log_a, norm_g, mix, s_stack, layer):
    def spec(width, off):
        return pl.BlockSpec((SEQ, width), lambda b, h: (b, off // width + h))

    ins = [z, z, z, z, log_a, norm_g.reshape(DEPTH, 1, GLA_DV), mix]
    in_specs = [spec(GLA_DK, OFF_Q), spec(GLA_DK, OFF_K), spec(GLA_DV, OFF_V), spec(GLA_DV, OFF_R),
                spec(GLA_DK, 0), pl.BlockSpec((None, 1, GLA_DV), lambda b, h: (layer, 0, 0)),
                pl.BlockSpec(memory_space=pl.ANY)]
    aliases = {6: 0}
    if s_stack is not None:
        ins.append(s_stack)
        in_specs.append(pl.BlockSpec(memory_space=pl.ANY))
        aliases[7] = 1
    return pl.pallas_call(
        _gla_prompt_kernel,
        grid=(BATCH, GLA_HEADS),
        in_specs=in_specs,
        out_specs=[
            pl.BlockSpec((SEQ, GLA_DV), lambda b, h: (b, D_POOL // GLA_DV + h)),
            pl.BlockSpec((None, 1, None, GLA_DK, GLA_DV), lambda b, h: (layer, b, h, 0, 0)),
        ],
        out_shape=[jax.ShapeDtypeStruct((N_TOK, D_MODEL), BF16),
                   jax.ShapeDtypeStruct((DEPTH, BATCH, GLA_HEADS, GLA_DK, GLA_DV), F32)],
        input_output_aliases=aliases,
        compiler_params=_params(("arbitrary", "arbitrary"), 52),
        name="gla_prompt",
    )(*ins)


def _gla_sample(z, log_a, norm_g, state_gla, mix, s_stack, layer):
    blk = N_P // N_S

    def spec(width, off):
        return pl.BlockSpec((N_S, width), lambda h, sb: (blk, off // width + h))

    ins = [z, z, z, z, log_a, norm_g.reshape(DEPTH, 1, GLA_DV), state_gla, mix]
    in_specs = [spec(GLA_DK, OFF_Q), spec(GLA_DK, OFF_K), spec(GLA_DV, OFF_V), spec(GLA_DV, OFF_R),
                spec(GLA_DK, 0), pl.BlockSpec((None, 1, GLA_DV), lambda h, sb: (layer, 0, 0)),
                pl.BlockSpec((None, GLA_SEQ_BLK, None, GLA_DK, GLA_DV), lambda h, sb: (layer, sb, h, 0, 0)),
                pl.BlockSpec(memory_space=pl.ANY)]
    aliases = {7: 0}
    if s_stack is not None:
        ins.append(s_stack)
        in_specs.append(pl.BlockSpec(memory_space=pl.ANY))
        aliases[8] = 1
    return pl.pallas_call(
        _gla_sample_kernel,
        grid=(GLA_HEADS, DEC_BATCH // GLA_SEQ_BLK),
        in_specs=in_specs,
        out_specs=[
            pl.BlockSpec((N_S, GLA_DV), lambda h, sb: (blk, D_POOL // GLA_DV + h)),
            pl.BlockSpec((None, GLA_SEQ_BLK, None, GLA_DK, GLA_DV), lambda h, sb: (layer, sb, h, 0, 0)),
        ],
        out_shape=[jax.ShapeDtypeStruct((N_TOK, D_MODEL), BF16),
                   jax.ShapeDtypeStruct((DEPTH, DEC_BATCH, GLA_HEADS, GLA_DK, GLA_DV), F32)],
        scratch_shapes=[pltpu.VMEM((N_S, GLA_DV), F32)],
        input_output_aliases=aliases,
        compiler_params=_params(("arbitrary", "arbitrary"), 52),
        name="gla_sample",
    )(*ins)


def _route_indices(mask, comb):
    mb = mask > 0.5
    cnt = jnp.sum(mb, axis=1, dtype=jnp.int32)
    tiles_e = (cnt + MOE_TM - 1) // MOE_TM
    tile_end = jnp.cumsum(tiles_e)
    tile_start = tile_end - tiles_e
    slot = tile_start[:, None] * MOE_TM + jnp.cumsum(mb, axis=1, dtype=jnp.int32) - 1
    e0 = jnp.argmax(mb, axis=0)
    e1 = N_EXPERTS - 1 - jnp.argmax(mb[::-1], axis=0)
    tok = jnp.arange(N_TOK, dtype=jnp.int32)
    slot0 = slot[e0, tok]
    slot1 = slot[e1, tok]
    w0 = comb[e0, tok][:, None]
    w1 = comb[e1, tok][:, None]
    tok_of_slot = jnp.zeros((N_SLOTS,), jnp.int32).at[slot0].set(tok).at[slot1].set(tok)
    tile = jnp.arange(N_SLOT_TILES, dtype=jnp.int32)
    tile_expert = jnp.minimum(jnp.sum(tile[:, None] >= tile_end[None, :], axis=1, dtype=jnp.int32), N_EXPERTS - 1)
    tile_valid = (tile < tile_end[-1]).astype(jnp.int32)
    prev = jnp.concatenate([jnp.full((1,), -1, jnp.int32), tile_expert[:-1]])
    tile_new = (tile_expert != prev).astype(jnp.int32)
    starts = jnp.where((tile_valid != 0) & (tile_new != 0), tile, N_SLOT_TILES)
    nxt = jnp.concatenate([lax.cummin(starts, reverse=True)[1:], jnp.full((1,), N_SLOT_TILES, jnp.int32)])
    next_expert = jnp.where(nxt < N_SLOT_TILES, tile_expert[jnp.minimum(nxt, N_SLOT_TILES - 1)], -1)
    return tok_of_slot, (tile_expert, tile_valid, tile_new, next_expert), slot0, slot1, w0, w1


def _gather_kernel(tok_ref, valid_ref, h_hbm, o_ref, buf, sem):
    t = pl.program_id(0)

    def start_tile(tile):
        slot = tile % 2
        base = tile * MOE_TM

        def issue(r, carry):
            tok = tok_ref[base + r]
            pltpu.make_async_copy(h_hbm.at[pl.ds(tok, 1)], buf.at[slot, pl.ds(r, 1)], sem.at[slot]).start()
            return carry

        lax.fori_loop(0, MOE_TM, issue, 0, unroll=8)

    @pl.when(t == 0)
    def _():
        start_tile(t)

    @pl.when(t + 1 < N_SLOT_TILES)
    def _():
        @pl.when(valid_ref[t + 1] != 0)
        def _():
            start_tile(t + 1)

    @pl.when(valid_ref[t] != 0)
    def _():
        slot = t % 2
        pltpu.make_async_copy(h_hbm.at[pl.ds(0, MOE_TM)], buf.at[slot], sem.at[slot]).wait()
        lo, hi = _unpack_bf16_pairs(buf[slot])
        o_ref[:, :HALF] = lo
        o_ref[:, HALF:] = hi

    @pl.when(valid_ref[t] == 0)
    def _():
        o_ref[...] = jnp.zeros(o_ref.shape, o_ref.dtype)


def _moe_gather(h2_packed, tok_of_slot, tile_valid):
    return pl.pallas_call(
        _gather_kernel,
        grid_spec=pltpu.PrefetchScalarGridSpec(
            num_scalar_prefetch=2,
            grid=(N_SLOT_TILES,),
            in_specs=[pl.BlockSpec(memory_space=pl.ANY)],
            out_specs=pl.BlockSpec((MOE_TM, D_MODEL), lambda t, *_: (t, 0)),
            scratch_shapes=[pltpu.VMEM((2, MOE_TM, HALF), jnp.uint32), pltpu.SemaphoreType.DMA((2,))],
        ),
        out_shape=jax.ShapeDtypeStruct((N_SLOTS, D_MODEL), BF16),
        compiler_params=_params(("arbitrary",), 32),
        name="moe_gather",
    )(tok_of_slot, tile_valid, h2_packed)


def _cast_rows(src, dst, rows_per_step=256):
    def step(i, carry):
        rows = pl.ds(pl.multiple_of(i * rows_per_step, rows_per_step), rows_per_step)
        dst[rows, :] = src[rows, :].astype(dst.dtype)
        return carry

    lax.fori_loop(0, src.shape[0] // rows_per_step, step, 0)


def _stream_expert_weights(t, cur_pass, n_pass, te_ref, tv_ref, tn_ref, nx_ref, count_ref, copies, on_arrival):
    @pl.when((cur_pass == 0) & (t == 0))
    def _():
        count_ref[0] = 0
        for c in copies(te_ref[0], 0, 0):
            c.start()

    @pl.when((tv_ref[t] != 0) & (tn_ref[t] != 0))
    def _():
        slot = count_ref[0] % 2
        for c in copies(te_ref[t], cur_pass, slot):
            c.wait()
        on_arrival(slot)
        nxt = nx_ref[t]

        @pl.when(nxt >= 0)
        def _():
            for c in copies(nxt, cur_pass, 1 - slot):
                c.start()

        @pl.when((nxt < 0) & (cur_pass + 1 < n_pass))
        def _():
            for c in copies(te_ref[0], cur_pass + 1, 1 - slot):
                c.start()

        count_ref[0] = count_ref[0] + 1


def _moe_up_kernel(te_ref, tv_ref, tn_ref, nx_ref, x_ref, w1_hbm, w3_hbm, o_ref,
                   wbuf, w1b, w3b, sem, count_ref, *, layer):
    f, t = pl.program_id(0), pl.program_id(1)

    def copies(e, p, slot):
        cols = pl.ds(pl.multiple_of(p * MOE_TF, MOE_TF), MOE_TF)
        return [pltpu.make_async_copy(w.at[layer, e, :, cols], wbuf.at[slot, k], sem.at[slot, k])
                for k, w in enumerate((w1_hbm, w3_hbm))]

    def on_arrival(slot):
        _cast_rows(wbuf.at[slot, 0], w1b)
        _cast_rows(wbuf.at[slot, 1], w3b)

    _stream_expert_weights(t, f, D_EXPERT // MOE_TF, te_ref, tv_ref, tn_ref, nx_ref, count_ref, copies, on_arrival)

    @pl.when(tv_ref[t] != 0)
    def _():
        x = x_ref[...]
        o_ref[...] = (_silu(_dot(x, w1b[...])) * _dot(x, w3b[...])).astype(o_ref.dtype)

    @pl.when(tv_ref[t] == 0)
    def _():
        o_ref[...] = jnp.zeros(o_ref.shape, o_ref.dtype)


def _moe_up(xs, w1, w3, tile_expert, tile_valid, tile_new, next_expert, layer):
    return pl.pallas_call(
        functools.partial(_moe_up_kernel, layer=layer),
        grid_spec=pltpu.PrefetchScalarGridSpec(
            num_scalar_prefetch=4,
            grid=(D_EXPERT // MOE_TF, N_SLOT_TILES),
            in_specs=[pl.BlockSpec((MOE_TM, D_MODEL), lambda f, t, *_: (t, 0)),
                      pl.BlockSpec(memory_space=pl.ANY), pl.BlockSpec(memory_space=pl.ANY)],
            out_specs=pl.BlockSpec((MOE_TM, MOE_TF), lambda f, t, *_: (t, f)),
            scratch_shapes=[pltpu.VMEM((2, 2, D_MODEL, MOE_TF), F32),
                            pltpu.VMEM((D_MODEL, MOE_TF), BF16), pltpu.VMEM((D_MODEL, MOE_TF), BF16),
                            pltpu.SemaphoreType.DMA((2, 2)), pltpu.SMEM((1,), jnp.int32)],
        ),
        out_shape=jax.ShapeDtypeStruct((N_SLOTS, D_EXPERT), BF16),
        compiler_params=_params(("arbitrary", "arbitrary"), 54),
        name="moe_up",
    )(tile_expert, tile_valid, tile_new, next_expert, xs, w1, w3)


def _moe_down_kernel(te_ref, tv_ref, tn_ref, nx_ref, h_ref, w2_hbm, o_ref, wbuf, w2b, sem, count_ref, *, layer):
    t = pl.program_id(0)

    def copies(e, p, slot):
        del p
        return [pltpu.make_async_copy(w2_hbm.at[layer, e], wbuf.at[slot], sem.at[slot])]

    def on_arrival(slot):
        _cast_rows(wbuf.at[slot], w2b)

    _stream_expert_weights(t, 0, 1, te_ref, tv_ref, tn_ref, nx_ref, count_ref, copies, on_arrival)

    @pl.when(tv_ref[t] != 0)
    def _():
        o_ref[...] = _dot(h_ref[...], w2b[...])

    @pl.when(tv_ref[t] == 0)
    def _():
        o_ref[...] = jnp.zeros(o_ref.shape, o_ref.dtype)


def _moe_down(hid, w2, tile_expert, tile_valid, tile_new, next_expert, layer):
    return pl.pallas_call(
        functools.partial(_moe_down_kernel, layer=layer),
        grid_spec=pltpu.PrefetchScalarGridSpec(
            num_scalar_prefetch=4,
            grid=(N_SLOT_TILES,),
            in_specs=[pl.BlockSpec((MOE_TM, D_EXPERT), lambda t, *_: (t, 0)), pl.BlockSpec(memory_space=pl.ANY)],
            out_specs=pl.BlockSpec((MOE_TM, D_MODEL), lambda t, *_: (t, 0)),
            scratch_shapes=[pltpu.VMEM((2, D_EXPERT, D_MODEL), F32), pltpu.VMEM((D_EXPERT, D_MODEL), BF16),
                            pltpu.SemaphoreType.DMA((2,)), pltpu.SMEM((1,), jnp.int32)],
        ),
        out_shape=jax.ShapeDtypeStruct((N_SLOTS, D_MODEL), F32),
        compiler_params=_params(("arbitrary",), 56),
        name="moe_down",
    )(tile_expert, tile_valid, tile_new, next_expert, hid, w2)


def _combine_kernel(s0_ref, s1_ref, y_hbm, x_ref, w0_ref, w1_ref, gate_ref, g_ref, *rest,
                    tile0, n_tiles, final):
    if final:
        out_ref, bufs, sems = rest
        tabs = [gate_ref]
    else:
        sc_ref, sh_ref, x2_ref, h_ref, bufs, sems = rest
        tabs = [gate_ref, sc_ref, sh_ref]
    step = pl.program_id(0)
    i = step + tile0

    def start_tile(j):
        par = j % 2
        base = (j + tile0) * TT

        def issue(r, carry):
            for k, s_ref in enumerate((s0_ref, s1_ref)):
                pltpu.make_async_copy(y_hbm.at[pl.ds(s_ref[base + r], 1)], bufs.at[par, k, pl.ds(r, 1)],
                                      sems.at[par, k]).start()
            return carry

        lax.fori_loop(0, TT, issue, 0, unroll=8)

    @pl.when(step == 0)
    def _():
        start_tile(step)

    @pl.when(step + 1 < n_tiles)
    def _():
        start_tile(step + 1)

    par = step % 2
    for k in range(TOP_K):
        pltpu.make_async_copy(y_hbm.at[pl.ds(0, TT)], bufs.at[par, k], sems.at[par, k]).wait()

    def body(mods):
        moe = w0_ref[...] * bufs[par, 0] + w1_ref[...] * bufs[par, 1]
        x2 = x_ref[...] + mods[0] * moe
        if final:
            out_ref[...] = _rms(x2, g_ref[...])
        else:
            x2_ref[...] = x2
            h_ref[...] = (_rms(x2, g_ref[...]) * (1 + mods[1]) + mods[2]).astype(h_ref.dtype)

    _by_tile_kind(i, tabs, body)


def _combine(y, x1, slot0, slot1, w0, w1, mod, layer, norm_g, *, final, tile0=0, n_tiles=N_PT + N_ST):
    def rows(width):
        return pl.BlockSpec((TT, width), lambda i, *_: (i + tile0, 0))

    in_specs = [pl.BlockSpec(memory_space=pl.ANY), rows(D_MODEL), rows(1), rows(1), _tab_spec(layer, 5)]
    ins = [y, x1, w0, w1, mod]
    if final:
        in_specs.append(pl.BlockSpec((1, D_MODEL), lambda i, *_: (0, 0)))
        ins.append(norm_g.reshape(1, D_MODEL))
        out_specs = _row_spec(D_MODEL)
        out_shape = jax.ShapeDtypeStruct((n_tiles * TT, D_MODEL), F32)
    else:
        in_specs += [_vec_spec(layer + 1), _tab_spec(layer + 1, 1), _tab_spec(layer + 1, 0)]
        ins += [norm_g.reshape(DEPTH, 1, D_MODEL), mod, mod]
        out_specs = [_row_spec(D_MODEL), _row_spec(D_MODEL)]
        out_shape = [jax.ShapeDtypeStruct((N_TOK, D_MODEL), F32), jax.ShapeDtypeStruct((N_TOK, D_MODEL), BF16)]
    return pl.pallas_call(
        functools.partial(_combine_kernel, tile0=tile0, n_tiles=n_tiles, final=final),
        grid_spec=pltpu.PrefetchScalarGridSpec(
            num_scalar_prefetch=2,
            grid=(n_tiles,),
            in_specs=in_specs,
            out_specs=out_specs,
            scratch_shapes=[pltpu.VMEM((2, TOP_K, TT, D_MODEL), F32), pltpu.SemaphoreType.DMA((2, TOP_K))],
        ),
        out_shape=out_shape,
        compiler_params=_params(("arbitrary",), 40),
        name="moe_combine_final" if final else "moe_combine",
    )(slot0, slot1, *ins)


def kernel(x_prompt, x_sample, c_prompt, c_sample, state_pool, state_gla, ada_w, ada_b, norm1_g, norm2_g, w_in, pool_w, pool_scale, gk_w2, gk_b, gla_norm_g, w_out, router_w, router_b, moe_w1, moe_w3, moe_w2, final_g):
    x = jnp.concatenate([x_prompt.reshape(N_P, D_MODEL),
                         x_sample.transpose(1, 0, 2).reshape(N_S, D_MODEL)], axis=0)
    mod = _adaln(jnp.concatenate([c_sample, c_prompt], axis=0), ada_w, ada_b)
    router_wt = router_w.T
    w_in_t = jnp.swapaxes(w_in, 1, 2)
    h = _norm_mod(x, mod, norm1_g, 0)

    pool_p, pool_s = [], []
    gla_p = gla_s = None
    for l in range(DEPTH):
        z = _matmul(h, w_in_t, l, OFF_G, "mm_in", w_transposed=True)
        log_a = _gate(h, w_in_t[l, OFF_G:, :], gk_w2, gk_b, l)

        prev_t = state_pool[l].transpose(1, 0, 2)
        mix = _pool_prompt(z, pool_w, pool_scale, l)
        mix = _pool_sample(z, prev_t, pool_w, pool_scale, mix, l)
        mix, gla_p = _gla_prompt(z, log_a, gla_norm_g, mix, gla_p, l)
        mix, gla_s = _gla_sample(z, log_a, gla_norm_g, state_gla, mix, gla_s, l)
        pool_p.append(jnp.stack([z[b * SEQ + SEQ - POOL_BUF:(b + 1) * SEQ, :D_POOL] for b in range(BATCH)]))
        u_s = z[N_P:, :D_POOL].reshape(DEC_SEQ, DEC_BATCH, D_POOL)
        pool_s.append(jnp.concatenate([prev_t[DEC_SEQ:], u_s], axis=0).transpose(1, 0, 2))

        mixed = _matmul(mix, w_out, l, D_MODEL, "mm_out")
        x1, h2, comb, mask = _post_mix(x, mixed, mod, norm2_g, router_wt, router_b, l)
        tok_of_slot, tiles, slot0, slot1, w0, w1 = _route_indices(mask, comb)
        xs = _moe_gather(h2, tok_of_slot, tiles[1])
        hid = _moe_up(xs, moe_w1, moe_w3, *tiles, l)
        y = _moe_down(hid, moe_w2, *tiles, l)
        if l + 1 < DEPTH:
            x, h = _combine(y, x1, slot0, slot1, w0, w1, mod, l, norm1_g, final=False)
        else:
            y_p = _combine(y, x1, slot0, slot1, w0, w1, mod, l, final_g, final=True, tile0=0, n_tiles=N_PT)
            y_s = _combine(y, x1, slot0, slot1, w0, w1, mod, l, final_g, final=True, tile0=N_PT, n_tiles=N_ST)

    return (y_p.reshape(BATCH, SEQ, D_MODEL),
            y_s.reshape(DEC_SEQ, DEC_BATCH, D_MODEL).transpose(1, 0, 2),
            jnp.stack(pool_p), gla_p, jnp.stack(pool_s), gla_s)
```

```python
import functools

import jax
import jax.numpy as jnp
from jax import lax
from jax.experimental import pallas as pl
from jax.experimental.pallas import tpu as pltpu

F32 = jnp.float32
BF16 = jnp.bfloat16

D_MODEL = 4096
BATCH = 4
SEQ = 2048
DEPTH = 4
DEC_BATCH = 128
DEC_SEQ = 8
PAST_LEN = 16384
D_POOL = 1024
POOL_WINDOWS = (2, 4, 8, 16)
POOL_GW = 256
POOL_BUF = 15
D_GLA_V = 3072
GLA_DV = 512
GLA_HEADS = 6
GLA_DK = 256
D_GLA_K = 1536
GATE_RANK = 16
GATE_NORM = 16.0
GLA_CHUNK = 64
OFF_Q = D_POOL
OFF_K = OFF_Q + D_GLA_K
OFF_V = OFF_K + D_GLA_K
OFF_R = OFF_V + D_GLA_V
OFF_G = OFF_R + D_GLA_V
N_EXPERTS = 16
N_EXPERT_GROUPS = 4
EXPERTS_PER_GROUP = 4
TOP_K = 2
D_EXPERT = 1024
N_MOD = 6
EPS = 1e-6

N_P = BATCH * SEQ
N_S = DEC_BATCH * DEC_SEQ
N_TOK = N_P + N_S
N_COND = DEC_BATCH + BATCH

TT = 128
N_PT = N_P // TT
N_ST = N_S // TT
TILES_PER_SEQ = SEQ // TT
MM_TM = 1024
MM_TN = 512
POOL_T = 256
POOL_HIST = 16
MOE_TM = 256
MOE_TF = 512
N_SLOT_TILES = (TOP_K * N_TOK) // MOE_TM + N_EXPERTS + 1
N_SLOTS = N_SLOT_TILES * MOE_TM
GLA_SEQ_BLK = 8

VMEM_MIB = 1024 * 1024


def _params(semantics, vmem_mib):
    return pltpu.CompilerParams(dimension_semantics=semantics, vmem_limit_bytes=vmem_mib * VMEM_MIB)


def _split3(x):
    hi = x.astype(BF16)
    r = x - hi.astype(F32)
    mid = r.astype(BF16)
    lo = (r - mid.astype(F32)).astype(BF16)
    return hi, mid, lo


def _dot(a, b):
    return jnp.dot(a, b, preferred_element_type=F32)


def _dot_nt(a, b):
    return lax.dot_general(a, b, (((1,), (1,)), ((), ())), preferred_element_type=F32)


def _dot_tn(a, b):
    return lax.dot_general(a, b, (((0,), (0,)), ((), ())), preferred_element_type=F32)


def _rms(x, g):
    return x * lax.rsqrt(jnp.mean(x * x, axis=-1, keepdims=True) + EPS) * g


def _silu(x):
    return x * jax.nn.sigmoid(x)


def _by_tile_kind(i, tabs, body):
    @pl.when(i < N_PT)
    def _():
        row = DEC_BATCH + i // TILES_PER_SEQ
        body([t[pl.ds(row, 1), :] for t in tabs])

    @pl.when(i >= N_PT)
    def _():
        body([t[0:DEC_BATCH, :] for t in tabs])


def _tab_spec(layer, j):
    return pl.BlockSpec((None, N_COND, D_MODEL), lambda i, *_: (layer, 0, j))


def _row_spec(width):
    return pl.BlockSpec((TT, width), lambda i, *_: (i, 0))


def _vec_spec(layer):
    return pl.BlockSpec((None, 1, D_MODEL), lambda i, *_: (layer, 0, 0))


def _mod_kernel(c_ref, w_ref, b_ref, o_ref):
    a = _silu(c_ref[...]).astype(BF16)
    o_ref[...] = _dot(a, w_ref[...].astype(BF16)) + b_ref[...]


def _adaln(c_all, ada_w, ada_b):
    tn = 1024
    n = ada_w.shape[-1]
    return pl.pallas_call(
        _mod_kernel,
        grid=(DEPTH, n // tn),
        in_specs=[
            pl.BlockSpec((N_COND, D_MODEL), lambda l, j: (0, 0)),
            pl.BlockSpec((None, D_MODEL, tn), lambda l, j: (l, 0, j)),
            pl.BlockSpec((None, 1, tn), lambda l, j: (l, 0, j)),
        ],
        out_specs=pl.BlockSpec((None, N_COND, tn), lambda l, j: (l, 0, j)),
        out_shape=jax.ShapeDtypeStruct((DEPTH, N_COND, n), F32),
        compiler_params=_params(("arbitrary", "arbitrary"), 52),
        name="adaln",
    )(c_all, ada_w, ada_b.reshape(DEPTH, 1, n))


def _norm_mod_kernel(x_ref, g_ref, sc_ref, sh_ref, h_ref):
    def body(mods):
        sc, sh = mods
        h_ref[...] = (_rms(x_ref[...], g_ref[...]) * (1 + sc) + sh).astype(h_ref.dtype)

    _by_tile_kind(pl.program_id(0), [sc_ref, sh_ref], body)


def _norm_mod(x, mod, norm_g, layer):
    return pl.pallas_call(
        _norm_mod_kernel,
        grid=(N_PT + N_ST,),
        in_specs=[_row_spec(D_MODEL), _vec_spec(layer), _tab_spec(layer, 1), _tab_spec(layer, 0)],
        out_specs=_row_spec(D_MODEL),
        out_shape=jax.ShapeDtypeStruct((N_TOK, D_MODEL), BF16),
        compiler_params=_params(("arbitrary",), 32),
        name="norm_mod",
    )(x, norm_g.reshape(DEPTH, 1, D_MODEL), mod, mod)


HALF = D_MODEL // 2
_HI16 = 0xFFFF0000


def _pack_bf16_pairs(h):
    bits = lax.bitcast_convert_type(h.astype(BF16).astype(F32), jnp.uint32)
    return lax.shift_right_logical(bits[:, :HALF], jnp.uint32(16)) | bits[:, HALF:]


def _unpack_bf16_pairs(u):
    lo = lax.bitcast_convert_type(lax.shift_left(u, jnp.uint32(16)), F32)
    hi = lax.bitcast_convert_type(u & jnp.uint32(_HI16), F32)
    return lo.astype(BF16), hi.astype(BF16)


def _route(h2, rwt, rb, comb_ref, mask_ref):
    h_hi, h_mid, _ = _split3(h2)
    w_hi, w_mid, _ = _split3(rwt)
    logits = _dot_nt(w_hi, h_hi) + _dot_nt(w_hi, h_mid) + _dot_nt(w_mid, h_hi)
    scores = jax.nn.sigmoid(logits)
    sel = scores + rb
    a = [sel[e:e + 1] for e in range(N_EXPERTS)]
    sc = [scores[e:e + 1] for e in range(N_EXPERTS)]

    best, best_g = None, None
    for g in range(N_EXPERT_GROUPS):
        a0, a1, a2, a3 = a[4 * g:4 * g + 4]
        hi01, lo01 = jnp.maximum(a0, a1), jnp.minimum(a0, a1)
        hi23, lo23 = jnp.maximum(a2, a3), jnp.minimum(a2, a3)
        gs = jnp.maximum(hi01, hi23) + jnp.maximum(jnp.minimum(hi01, hi23), jnp.maximum(lo01, lo23))
        if g == 0:
            best, best_g = gs, jnp.zeros(gs.shape, jnp.int32)
        else:
            upd = gs > best
            best = jnp.where(upd, gs, best)
            best_g = jnp.where(upd, g, best_g)

    chosen = []
    for e in range(N_EXPERTS):
        g = e // EXPERTS_PER_GROUP
        rank = jnp.zeros(a[e].shape, jnp.int32)
        for j in range(4 * g, 4 * g + 4):
            if j == e:
                continue
            ahead = (a[j] > a[e]) | ((a[j] == a[e]) & (j < e))
            rank = rank + ahead.astype(jnp.int32)
        chosen.append((best_g == g) & (rank < TOP_K))

    denom = jnp.zeros(a[0].shape, F32)
    for e in range(N_EXPERTS):
        denom = denom + jnp.where(chosen[e], sc[e], 0.0)
    for e in range(N_EXPERTS):
        comb_ref[e:e + 1, :] = jnp.where(chosen[e], sc[e] / denom, 0.0)
        mask_ref[e:e + 1, :] = jnp.where(chosen[e], 1.0, 0.0)


def _post_mix_kernel(x_ref, m_ref, g_ref, rwt_ref, rb_ref, gate_ref, sc_ref, sh_ref,
                     x1_ref, hp_ref, comb_ref, mask_ref):
    def body(mods):
        gate, sc, sh = mods
        x1 = x_ref[...] + gate * m_ref[...]
        x1_ref[...] = x1
        h2 = _rms(x1, g_ref[...]) * (1 + sc) + sh
        hp_ref[...] = _pack_bf16_pairs(h2)
        _route(h2, rwt_ref[...], rb_ref[...], comb_ref, mask_ref)

    _by_tile_kind(pl.program_id(0), [gate_ref, sc_ref, sh_ref], body)


def _post_mix(x, mixed, mod, norm_g, router_wt, router_b, layer):
    return pl.pallas_call(
        _post_mix_kernel,
        grid=(N_PT + N_ST,),
        in_specs=[_row_spec(D_MODEL), _row_spec(D_MODEL), _vec_spec(layer),
                  pl.BlockSpec((N_EXPERTS, D_MODEL), lambda i: (0, 0)),
                  pl.BlockSpec((N_EXPERTS, 1), lambda i: (0, 0)),
                  _tab_spec(layer, 2), _tab_spec(layer, 4), _tab_spec(layer, 3)],
        out_specs=[_row_spec(D_MODEL), _row_spec(HALF),
                   pl.BlockSpec((N_EXPERTS, TT), lambda i: (0, i)),
                   pl.BlockSpec((N_EXPERTS, TT), lambda i: (0, i))],
        out_shape=[jax.ShapeDtypeStruct((N_TOK, D_MODEL), F32),
                   jax.ShapeDtypeStruct((N_TOK, HALF), jnp.uint32),
                   jax.ShapeDtypeStruct((N_EXPERTS, N_TOK), F32),
                   jax.ShapeDtypeStruct((N_EXPERTS, N_TOK), F32)],
        compiler_params=_params(("arbitrary",), 40),
        name="post_mix",
    )(x, mixed, norm_g.reshape(DEPTH, 1, D_MODEL), router_wt, router_b.reshape(N_EXPERTS, 1), mod, mod, mod)


def _mm_kernel(a_ref, w_ref, o_ref, wb_ref, *, w_transposed):
    @pl.when(pl.program_id(1) == 0)
    def _():
        w = w_ref[...]
        wb_ref[...] = (w.T if w_transposed else w).astype(BF16)

    o_ref[...] = _dot(a_ref[...], wb_ref[...]).astype(o_ref.dtype)


def _matmul(a, w, layer, n_cols, name, w_transposed=False):
    m, k = a.shape
    if w_transposed:
        w_spec = pl.BlockSpec((None, MM_TN, k), lambda j, i: (layer, j, 0))
    else:
        w_spec = pl.BlockSpec((None, k, MM_TN), lambda j, i: (layer, 0, j))
    return pl.pallas_call(
        functools.partial(_mm_kernel, w_transposed=w_transposed),
        grid=(n_cols // MM_TN, m // MM_TM),
        in_specs=[pl.BlockSpec((MM_TM, k), lambda j, i: (i, 0)), w_spec],
        out_specs=pl.BlockSpec((MM_TM, MM_TN), lambda j, i: (i, j)),
        out_shape=jax.ShapeDtypeStruct((m, n_cols), F32),
        scratch_shapes=[pltpu.VMEM((k, MM_TN), BF16)],
        compiler_params=_params(("arbitrary", "arbitrary"), 48),
        name=name,
    )(a, w)


def _gate_kernel(h_ref, wg_ref, w2_ref, b_ref, o_ref):
    wg = wg_ref[...].astype(BF16)
    z3 = _dot_nt(h_ref[...], jnp.concatenate([wg, wg, wg], axis=0))
    z_hi = z3.astype(BF16).astype(F32)
    lane = lax.broadcasted_iota(jnp.int32, z3.shape, 1)
    lhs = jnp.where(lane < 2 * GATE_RANK, z_hi, z3 - z_hi).astype(BF16)
    w_hi, w_mid, _ = _split3(w2_ref[...])
    logit = _dot(lhs, jnp.concatenate([w_hi, w_mid, w_hi], axis=0)) + b_ref[...]
    o_ref[...] = (jnp.minimum(logit, 0.0) - jnp.log(1.0 + jnp.exp(-jnp.abs(logit)))) * (1.0 / GATE_NORM)


def _gate(h, w_gate, gk_w2, gk_b, layer):
    tm = 512
    return pl.pallas_call(
        _gate_kernel,
        grid=(N_TOK // tm,),
        in_specs=[
            pl.BlockSpec((tm, D_MODEL), lambda i: (i, 0)),
            pl.BlockSpec((GATE_RANK, D_MODEL), lambda i: (0, 0)),
            pl.BlockSpec((None, GATE_RANK, D_GLA_K), lambda i: (layer, 0, 0)),
            pl.BlockSpec((None, 1, D_GLA_K), lambda i: (layer, 0, 0)),
        ],
        out_specs=pl.BlockSpec((tm, D_GLA_K), lambda i: (i, 0)),
        out_shape=jax.ShapeDtypeStruct((N_TOK, D_GLA_K), F32),
        compiler_params=_params(("arbitrary",), 32),
        name="gla_gate",
    )(h, w_gate, gk_w2, gk_b.reshape(DEPTH, 1, D_GLA_K))


def _pool_windows(load, pos, pw_ref, ps_ref, o_ref):
    for g, w in enumerate(POOL_WINDOWS):
        cs = slice(g * POOL_GW, (g + 1) * POOL_GW)
        cur = load(0, cs)
        win = cur
        for k in range(1, w):
            win = win + load(k, cs)
        cnt = jnp.minimum(pos + 1, w).astype(F32)
        d = win / cnt - cur
        y = _dot(d.astype(BF16), pw_ref[g].astype(BF16))
        o_ref[:, cs] = (y * ps_ref[:, cs]).astype(o_ref.dtype)


def _pool_prompt_kernel(u_ref, pw_ref, ps_ref, o_ref, ext_ref):
    c = pl.program_id(1)

    @pl.when(c == 0)
    def _():
        ext_ref[0:POOL_HIST, :] = jnp.zeros((POOL_HIST, D_POOL), F32)

    @pl.when(c > 0)
    def _():
        ext_ref[0:POOL_HIST, :] = ext_ref[POOL_T:POOL_T + POOL_HIST, :]

    ext_ref[POOL_HIST:POOL_HIST + POOL_T, :] = u_ref[...]
    pos = c * POOL_T + lax.broadcasted_iota(jnp.int32, (POOL_T, POOL_GW), 0)

    def load(k, cs):
        return ext_ref[POOL_HIST - k:POOL_HIST - k + POOL_T, cs]

    _pool_windows(load, pos, pw_ref, ps_ref, o_ref)


def _pool_sample_kernel(u_ref, prev_ref, pw_ref, ps_ref, _mix_in, o_ref):
    def load(k, cs):
        outs = []
        for t in range(DEC_SEQ):
            p = t - k
            if p >= 0:
                outs.append(u_ref[p * DEC_BATCH:(p + 1) * DEC_BATCH, cs])
            else:
                outs.append(prev_ref[POOL_BUF + p, :, cs])
        return jnp.concatenate(outs, axis=0)

    row = lax.broadcasted_iota(jnp.int32, (N_S, POOL_GW), 0)
    pos = PAST_LEN + lax.shift_right_logical(row, DEC_BATCH.bit_length() - 1)
    _pool_windows(load, pos, pw_ref, ps_ref, o_ref)


def _pool_prompt(z, pool_w, pool_scale, layer):
    steps = SEQ // POOL_T
    return pl.pallas_call(
        _pool_prompt_kernel,
        grid=(BATCH, steps),
        in_specs=[
            pl.BlockSpec((POOL_T, D_POOL), lambda b, c: (b * steps + c, 0)),
            pl.BlockSpec((None, len(POOL_WINDOWS), POOL_GW, POOL_GW), lambda b, c: (layer, 0, 0, 0)),
            pl.BlockSpec((None, 1, D_POOL), lambda b, c: (layer, 0, 0)),
        ],
        out_specs=pl.BlockSpec((POOL_T, D_POOL), lambda b, c: (b * steps + c, 0)),
        out_shape=jax.ShapeDtypeStruct((N_TOK, D_MODEL), BF16),
        scratch_shapes=[pltpu.VMEM((POOL_HIST + POOL_T, D_POOL), F32)],
        compiler_params=_params(("arbitrary", "arbitrary"), 32),
        name="pool_prompt",
    )(z, pool_w, pool_scale.reshape(DEPTH, 1, D_POOL))


def _pool_sample(z, prev_t, pool_w, pool_scale, mix, layer):
    blk = N_P // N_S
    return pl.pallas_call(
        _pool_sample_kernel,
        grid=(1,),
        in_specs=[
            pl.BlockSpec((N_S, D_POOL), lambda i: (blk, 0)),
            pl.BlockSpec((POOL_BUF, DEC_BATCH, D_POOL), lambda i: (0, 0, 0)),
            pl.BlockSpec((None, len(POOL_WINDOWS), POOL_GW, POOL_GW), lambda i: (layer, 0, 0, 0)),
            pl.BlockSpec((None, 1, D_POOL), lambda i: (layer, 0, 0)),
            pl.BlockSpec(memory_space=pl.ANY),
        ],
        out_specs=pl.BlockSpec((N_S, D_POOL), lambda i: (blk, 0)),
        out_shape=jax.ShapeDtypeStruct((N_TOK, D_MODEL), BF16),
        input_output_aliases={4: 0},
        compiler_params=_params(("arbitrary",), 48),
        name="pool_sample",
    )(z, prev_t, pool_w, pool_scale.reshape(DEPTH, 1, D_POOL), mix)


def _gla_chunk(la, q, k, v, r, g, s_ref, nseq, n_sub, seq_len):
    c = GLA_CHUNK
    ct = c * n_sub
    mid = seq_len // 2
    assert n_sub * nseq <= 8 and (n_sub == 1 or nseq == 1)

    def group_pos(idx):
        sub = lax.shift_right_logical(idx, c.bit_length() - 1)
        within = idx & (c - 1)
        return sub * nseq + (within & (nseq - 1)), lax.shift_right_logical(within, nseq.bit_length() - 1)

    ri = lax.broadcasted_iota(jnp.int32, (ct + 16, ct), 0)
    gj, pj = group_pos(lax.broadcasted_iota(jnp.int32, (ct + 16, ct), 1))
    gi, pi = group_pos(ri)
    extra = ri - ct
    in_sum = (((ri < ct) & (gi == gj) & (pj <= pi))
              | ((ri >= ct) & ((extra & 7) == gj) & ((extra >= 8) | (pj <= mid))))
    sums = jnp.where(in_sum, 1.0, 0.0).astype(BF16)

    la_hi, la_mid, _ = _split3(la)
    b3 = _dot(sums, la_hi) + _dot(sums, la_mid)
    b, mids, tots = b3[0:ct], b3[ct:ct + 8], b3[ct + 8:ct + 16]
    tots_col = tots.T

    def per_row(t8):
        if nseq == 1:
            return jnp.concatenate([jnp.broadcast_to(t8[s:s + 1], (c, GLA_DK)) for s in range(n_sub)], axis=0)
        return jnp.concatenate([t8] * (c // 8), axis=0)

    b_mid, b_last = per_row(mids), per_row(tots)
    gi, pi = group_pos(lax.broadcasted_iota(jnp.int32, (ct, ct), 0))
    gj, pj = group_pos(lax.broadcasted_iota(jnp.int32, (ct, ct), 1))
    row_g, _ = group_pos(lax.broadcasted_iota(jnp.int32, (ct, 1), 0))

    qs = q * (GLA_DK ** -0.5)
    q_in = qs * jnp.exp(b)
    k_out = k * jnp.exp(b_last - b)
    a = _dot_nt((qs * jnp.exp(b - b_mid)).astype(BF16), (k * jnp.exp(b_mid - b)).astype(BF16))
    a = jnp.where((gi == gj) & (pj <= pi), a, 0.0)
    if n_sub == 2:
        a = jnp.where(gi > gj, _dot_nt(q_in.astype(BF16), k_out.astype(BF16)), a)
    vb = v.astype(BF16)
    o = _dot(a.astype(BF16), vb)
    if n_sub == 2:
        st = s_ref[0]
        q_s = jnp.where(row_g == 1, q_in * jnp.exp(tots[0:1]), q_in)
        k_s = jnp.where(row_g == 0, k_out * jnp.exp(tots[1:2]), k_out)
        o = o + _dot(q_s.astype(BF16), st.astype(BF16))
        s_ref[0] = jnp.exp(tots_col[:, 0:1] + tots_col[:, 1:2]) * st + _dot_tn(k_s.astype(BF16), vb)
    else:
        q_inter = q_in.astype(BF16)
        o_inter = None
        for s in range(nseq):
            st = s_ref[s]
            oi = _dot(q_inter, st.astype(BF16))
            o_inter = oi if o_inter is None else jnp.where(row_g == s, oi, o_inter)
            ks = k_out if nseq == 1 else jnp.where(row_g == s, k_out, 0.0)
            s_ref[s] = jnp.exp(tots_col[:, s:s + 1]) * st + _dot_tn(ks.astype(BF16), vb)
        o = o + o_inter
    return _rms(o, g) * _silu(r)


GLA_SUB = 2


def _gla_prompt_kernel(q_ref, k_ref, v_ref, r_ref, la_ref, g_ref, *rest):
    o_ref, s_ref = rest[-2:]
    s_ref[...] = jnp.zeros(s_ref.shape, F32)
    n_rows = GLA_SUB * GLA_CHUNK

    def chunk(ci, carry):
        rows = pl.ds(pl.multiple_of(ci * n_rows, n_rows), n_rows)
        o = _gla_chunk(la_ref[rows, :], q_ref[rows, :], k_ref[rows, :], v_ref[rows, :], r_ref[rows, :],
                       g_ref[...], s_ref, 1, GLA_SUB, GLA_CHUNK)
        o_ref[rows, :] = o.astype(o_ref.dtype)
        return carry

    lax.fori_loop(0, SEQ // n_rows, chunk, 0, unroll=4)


def _gla_sample_kernel(q_ref, k_ref, v_ref, r_ref, la_ref, g_ref, s0_ref, *rest):
    o_ref, s_ref, o_scr = rest[-3:]
    sb = pl.program_id(1)
    s_ref[...] = s0_ref[...]
    base = pl.multiple_of(sb * GLA_SEQ_BLK, GLA_SEQ_BLK)

    def rows(ref):
        return jnp.concatenate(
            [ref[pl.ds(t * DEC_BATCH + base, GLA_SEQ_BLK), :] for t in range(DEC_SEQ)], axis=0)

    o = _gla_chunk(rows(la_ref), rows(q_ref), rows(k_ref), rows(v_ref), rows(r_ref),
                   g_ref[...], s_ref, GLA_SEQ_BLK, 1, DEC_SEQ)
    for t in range(DEC_SEQ):
        o_scr[pl.ds(t * DEC_BATCH + base, GLA_SEQ_BLK), :] = o[t * GLA_SEQ_BLK:(t + 1) * GLA_SEQ_BLK]

    @pl.when(sb == DEC_BATCH // GLA_SEQ_BLK - 1)
    def _():
        o_ref[...] = o_scr[...].astype(o_ref.dtype)


def _gla_prompt(z, log_a, norm_g, mix, s_stack, layer):
    def spec(width, off):
        return pl.BlockSpec((SEQ, width), lambda b, h: (b, off // width + h))

    ins = [z, z, z, z, log_a, norm_g.reshape(DEPTH, 1, GLA_DV), mix]
    in_specs = [spec(GLA_DK, OFF_Q), spec(GLA_DK, OFF_K), spec(GLA_DV, OFF_V), spec(GLA_DV, OFF_R),
                spec(GLA_DK, 0), pl.BlockSpec((None, 1, GLA_DV), lambda b, h: (layer, 0, 0)),
                pl.BlockSpec(memory_space=pl.ANY)]
    aliases = {6: 0}
    if s_stack is not None:
        ins.append(s_stack)
        in_specs.append(pl.BlockSpec(memory_space=pl.ANY))
        aliases[7] = 1
    return pl.pallas_call(
        _gla_prompt_kernel,
        grid=(BATCH, GLA_HEADS),
        in_specs=in_specs,
        out_specs=[
            pl.BlockSpec((SEQ, GLA_DV), lambda b, h: (b, D_POOL // GLA_DV + h)),
            pl.BlockSpec((None, 1, None, GLA_DK, GLA_DV), lambda b, h: (layer, b, h, 0, 0)),
        ],
        out_shape=[jax.ShapeDtypeStruct((N_TOK, D_MODEL), BF16),
                   jax.ShapeDtypeStruct((DEPTH, BATCH, GLA_HEADS, GLA_DK, GLA_DV), F32)],
        input_output_aliases=aliases,
        compiler_params=_params(("arbitrary", "arbitrary"), 52),
        name="gla_prompt",
    )(*ins)


def _gla_sample(z, log_a, norm_g, state_gla, mix, s_stack, layer):
    blk = N_P // N_S

    def spec(width, off):
        return pl.BlockSpec((N_S, width), lambda h, sb: (blk, off // width + h))

    ins = [z, z, z, z, log_a, norm_g.reshape(DEPTH, 1, GLA_DV), state_gla, mix]
    in_specs = [spec(GLA_DK, OFF_Q), spec(GLA_DK, OFF_K), spec(GLA_DV, OFF_V), spec(GLA_DV, OFF_R),
                spec(GLA_DK, 0), pl.BlockSpec((None, 1, GLA_DV), lambda h, sb: (layer, 0, 0)),
                pl.BlockSpec((None, GLA_SEQ_BLK, None, GLA_DK, GLA_DV), lambda h, sb: (layer, sb, h, 0, 0)),
                pl.BlockSpec(memory_space=pl.ANY)]
    aliases = {7: 0}
    if s_stack is not None:
        ins.append(s_stack)
        in_specs.append(pl.BlockSpec(memory_space=pl.ANY))
        aliases[8] = 1
    return pl.pallas_call(
        _gla_sample_kernel,
        grid=(GLA_HEADS, DEC_BATCH // GLA_SEQ_BLK),
        in_specs=in_specs,
        out_specs=[
            pl.BlockSpec((N_S, GLA_DV), lambda h, sb: (blk, D_POOL // GLA_DV + h)),
            pl.BlockSpec((None, GLA_SEQ_BLK, None, GLA_DK, GLA_DV), lambda h, sb: (layer, sb, h, 0, 0)),
        ],
        out_shape=[jax.ShapeDtypeStruct((N_TOK, D_MODEL), BF16),
                   jax.ShapeDtypeStruct((DEPTH, DEC_BATCH, GLA_HEADS, GLA_DK, GLA_DV), F32)],
        scratch_shapes=[pltpu.VMEM((N_S, GLA_DV), F32)],
        input_output_aliases=aliases,
        compiler_params=_params(("arbitrary", "arbitrary"), 52),
        name="gla_sample",
    )(*ins)


def _route_indices(mask, comb):
    mb = mask > 0.5
    cnt = jnp.sum(mb, axis=1, dtype=jnp.int32)
    tiles_e = (cnt + MOE_TM - 1) // MOE_TM
    tile_end = jnp.cumsum(tiles_e)
    tile_start = tile_end - tiles_e
    slot = tile_start[:, None] * MOE_TM + jnp.cumsum(mb, axis=1, dtype=jnp.int32) - 1
    e0 = jnp.argmax(mb, axis=0)
    e1 = N_EXPERTS - 1 - jnp.argmax(mb[::-1], axis=0)
    tok = jnp.arange(N_TOK, dtype=jnp.int32)
    slot0 = slot[e0, tok]
    slot1 = slot[e1, tok]
    w0 = comb[e0, tok][:, None]
    w1 = comb[e1, tok][:, None]
    tok_of_slot = jnp.zeros((N_SLOTS,), jnp.int32).at[slot0].set(tok).at[slot1].set(tok)
    tile = jnp.arange(N_SLOT_TILES, dtype=jnp.int32)
    tile_expert = jnp.minimum(jnp.sum(tile[:, None] >= tile_end[None, :], axis=1, dtype=jnp.int32), N_EXPERTS - 1)
    tile_valid = (tile < tile_end[-1]).astype(jnp.int32)
    prev = jnp.concatenate([jnp.full((1,), -1, jnp.int32), tile_expert[:-1]])
    tile_new = (tile_expert != prev).astype(jnp.int32)
    starts = jnp.where((tile_valid != 0) & (tile_new != 0), tile, N_SLOT_TILES)
    nxt = jnp.concatenate([lax.cummin(starts, reverse=True)[1:], jnp.full((1,), N_SLOT_TILES, jnp.int32)])
    next_expert = jnp.where(nxt < N_SLOT_TILES, tile_expert[jnp.minimum(nxt, N_SLOT_TILES - 1)], -1)
    return tok_of_slot, (tile_expert, tile_valid, tile_new, next_expert), slot0, slot1, w0, w1


def _cast_rows(src, dst, rows_per_step=256):
    def step(i, carry):
        rows = pl.ds(pl.multiple_of(i * rows_per_step, rows_per_step), rows_per_step)
        dst[rows, :] = src[rows, :].astype(dst.dtype)
        return carry

    lax.fori_loop(0, src.shape[0] // rows_per_step, step, 0)


def _stream_expert_weights(t, cur_pass, n_pass, te_ref, tv_ref, tn_ref, nx_ref, count_ref, copies, on_arrival):
    @pl.when((cur_pass == 0) & (t == 0))
    def _():
        count_ref[0] = 0
        for c in copies(te_ref[0], 0, 0):
            c.start()

    @pl.when((tv_ref[t] != 0) & (tn_ref[t] != 0))
    def _():
        slot = count_ref[0] % 2
        for c in copies(te_ref[t], cur_pass, slot):
            c.wait()
        on_arrival(slot)
        nxt = nx_ref[t]

        @pl.when(nxt >= 0)
        def _():
            for c in copies(nxt, cur_pass, 1 - slot):
                c.start()

        @pl.when((nxt < 0) & (cur_pass + 1 < n_pass))
        def _():
            for c in copies(te_ref[0], cur_pass + 1, 1 - slot):
                c.start()

        count_ref[0] = count_ref[0] + 1


def _moe_up_kernel(tok_ref, te_ref, tv_ref, tn_ref, nx_ref, h_hbm, w1_hbm, w3_hbm, o_ref,
                   gbuf, xb, wbuf, w1b, w3b, gsem, wsem, count_ref, *, layer):
    f, t = pl.program_id(0), pl.program_id(1)
    slot = t % 2

    def row_copy(tile, r, to_slot):
        tok = tok_ref[tile * MOE_TM + r]
        return pltpu.make_async_copy(h_hbm.at[pl.ds(tok, 1)], gbuf.at[to_slot, pl.ds(r, 1)], gsem.at[to_slot])

    def wait_rows(of_slot):
        pltpu.make_async_copy(h_hbm.at[pl.ds(0, MOE_TM)], gbuf.at[of_slot], gsem.at[of_slot]).wait()

    @pl.when(t == 0)
    def _():
        def issue(r, carry):
            row_copy(t, r, 0).start()
            return carry

        lax.fori_loop(0, MOE_TM, issue, 0, unroll=8)

    def copies(e, p, wslot):
        cols = pl.ds(pl.multiple_of(p * MOE_TF, MOE_TF), MOE_TF)
        return [pltpu.make_async_copy(w.at[layer, e, :, cols], wbuf.at[wslot, k], wsem.at[wslot, k])
                for k, w in enumerate((w1_hbm, w3_hbm))]

    def on_arrival(wslot):
        _cast_rows(wbuf.at[wslot, 0], w1b)
        _cast_rows(wbuf.at[wslot, 1], w3b)

    _stream_expert_weights(t, f, D_EXPERT // MOE_TF, te_ref, tv_ref, tn_ref, nx_ref, count_ref, copies, on_arrival)

    @pl.when(tv_ref[t] != 0)
    def _():
        wait_rows(slot)
        lo, hi = _unpack_bf16_pairs(gbuf[slot])
        xb[:, :HALF] = lo
        xb[:, HALF:] = hi
        for r in range(MOE_TM):
            row_copy(t + 1, r, 1 - slot).start()
        x = xb[...]
        o_ref[...] = (_silu(_dot(x, w1b[...])) * _dot(x, w3b[...])).astype(o_ref.dtype)

    @pl.when(tv_ref[t] == 0)
    def _():
        o_ref[...] = jnp.zeros(o_ref.shape, o_ref.dtype)

        @pl.when(t > 0)
        def _():
            @pl.when(tv_ref[t - 1] != 0)
            def _():
                wait_rows(slot)


def _moe_up(h2_packed, tok_of_slot, w1, w3, tile_expert, tile_valid, tile_new, next_expert, layer):
    return pl.pallas_call(
        functools.partial(_moe_up_kernel, layer=layer),
        grid_spec=pltpu.PrefetchScalarGridSpec(
            num_scalar_prefetch=5,
            grid=(D_EXPERT // MOE_TF, N_SLOT_TILES),
            in_specs=[pl.BlockSpec(memory_space=pl.ANY)] * 3,
            out_specs=pl.BlockSpec((MOE_TM, MOE_TF), lambda f, t, *_: (t, f)),
            scratch_shapes=[pltpu.VMEM((2, MOE_TM, HALF), jnp.uint32), pltpu.VMEM((MOE_TM, D_MODEL), BF16),
                            pltpu.VMEM((2, 2, D_MODEL, MOE_TF), F32),
                            pltpu.VMEM((D_MODEL, MOE_TF), BF16), pltpu.VMEM((D_MODEL, MOE_TF), BF16),
                            pltpu.SemaphoreType.DMA((2,)), pltpu.SemaphoreType.DMA((2, 2)),
                            pltpu.SMEM((1,), jnp.int32)],
        ),
        out_shape=jax.ShapeDtypeStruct((N_SLOTS, D_EXPERT), BF16),
        compiler_params=_params(("arbitrary", "arbitrary"), 54),
        name="moe_up",
    )(tok_of_slot, tile_expert, tile_valid, tile_new, next_expert, h2_packed, w1, w3)


def _moe_down_kernel(te_ref, tv_ref, tn_ref, nx_ref, h_ref, w2_hbm, o_ref, wbuf, w2b, sem, count_ref, *, layer):
    t = pl.program_id(0)

    def copies(e, p, slot):
        del p
        return [pltpu.make_async_copy(w2_hbm.at[layer, e], wbuf.at[slot], sem.at[slot])]

    def on_arrival(slot):
        _cast_rows(wbuf.at[slot], w2b)

    _stream_expert_weights(t, 0, 1, te_ref, tv_ref, tn_ref, nx_ref, count_ref, copies, on_arrival)

    @pl.when(tv_ref[t] != 0)
    def _():
        o_ref[...] = _dot(h_ref[...], w2b[...])

    @pl.when(tv_ref[t] == 0)
    def _():
        o_ref[...] = jnp.zeros(o_ref.shape, o_ref.dtype)


def _moe_down(hid, w2, tile_expert, tile_valid, tile_new, next_expert, layer):
    return pl.pallas_call(
        functools.partial(_moe_down_kernel, layer=layer),
        grid_spec=pltpu.PrefetchScalarGridSpec(
            num_scalar_prefetch=4,
            grid=(N_SLOT_TILES,),
            in_specs=[pl.BlockSpec((MOE_TM, D_EXPERT), lambda t, *_: (t, 0)), pl.BlockSpec(memory_space=pl.ANY)],
            out_specs=pl.BlockSpec((MOE_TM, D_MODEL), lambda t, *_: (t, 0)),
            scratch_shapes=[pltpu.VMEM((2, D_EXPERT, D_MODEL), F32), pltpu.VMEM((D_EXPERT, D_MODEL), BF16),
                            pltpu.SemaphoreType.DMA((2,)), pltpu.SMEM((1,), jnp.int32)],
        ),
        out_shape=jax.ShapeDtypeStruct((N_SLOTS, D_MODEL), F32),
        compiler_params=_params(("arbitrary",), 56),
        name="moe_down",
    )(tile_expert, tile_valid, tile_new, next_expert, hid, w2)


def _combine_kernel(s0_ref, s1_ref, y_hbm, x_ref, w0_ref, w1_ref, gate_ref, g_ref, *rest,
                    tile0, n_tiles, final):
    if final:
        out_ref, bufs, sems = rest
        tabs = [gate_ref]
    else:
        sc_ref, sh_ref, x2_ref, h_ref, bufs, sems = rest
        tabs = [gate_ref, sc_ref, sh_ref]
    step = pl.program_id(0)
    i = step + tile0

    def start_tile(j):
        par = j % 2
        base = (j + tile0) * TT

        def issue(r, carry):
            for k, s_ref in enumerate((s0_ref, s1_ref)):
                pltpu.make_async_copy(y_hbm.at[pl.ds(s_ref[base + r], 1)], bufs.at[par, k, pl.ds(r, 1)],
                                      sems.at[par, k]).start()
            return carry

        lax.fori_loop(0, TT, issue, 0, unroll=8)

    @pl.when(step == 0)
    def _():
        start_tile(step)

    @pl.when(step + 1 < n_tiles)
    def _():
        start_tile(step + 1)

    par = step % 2
    for k in range(TOP_K):
        pltpu.make_async_copy(y_hbm.at[pl.ds(0, TT)], bufs.at[par, k], sems.at[par, k]).wait()

    def body(mods):
        moe = w0_ref[...] * bufs[par, 0] + w1_ref[...] * bufs[par, 1]
        x2 = x_ref[...] + mods[0] * moe
        if final:
            out_ref[...] = _rms(x2, g_ref[...])
        else:
            x2_ref[...] = x2
            h_ref[...] = (_rms(x2, g_ref[...]) * (1 + mods[1]) + mods[2]).astype(h_ref.dtype)

    _by_tile_kind(i, tabs, body)


def _combine(y, x1, slot0, slot1, w0, w1, mod, layer, norm_g, *, final, tile0=0, n_tiles=N_PT + N_ST):
    def rows(width):
        return pl.BlockSpec((TT, width), lambda i, *_: (i + tile0, 0))

    in_specs = [pl.BlockSpec(memory_space=pl.ANY), rows(D_MODEL), rows(1), rows(1), _tab_spec(layer, 5)]
    ins = [y, x1, w0, w1, mod]
    if final:
        in_specs.append(pl.BlockSpec((1, D_MODEL), lambda i, *_: (0, 0)))
        ins.append(norm_g.reshape(1, D_MODEL))
        out_specs = _row_spec(D_MODEL)
        out_shape = jax.ShapeDtypeStruct((n_tiles * TT, D_MODEL), F32)
    else:
        in_specs += [_vec_spec(layer + 1), _tab_spec(layer + 1, 1), _tab_spec(layer + 1, 0)]
        ins += [norm_g.reshape(DEPTH, 1, D_MODEL), mod, mod]
        out_specs = [_row_spec(D_MODEL), _row_spec(D_MODEL)]
        out_shape = [jax.ShapeDtypeStruct((N_TOK, D_MODEL), F32), jax.ShapeDtypeStruct((N_TOK, D_MODEL), BF16)]
    return pl.pallas_call(
        functools.partial(_combine_kernel, tile0=tile0, n_tiles=n_tiles, final=final),
        grid_spec=pltpu.PrefetchScalarGridSpec(
            num_scalar_prefetch=2,
            grid=(n_tiles,),
            in_specs=in_specs,
            out_specs=out_specs,
            scratch_shapes=[pltpu.VMEM((2, TOP_K, TT, D_MODEL), F32), pltpu.SemaphoreType.DMA((2, TOP_K))],
        ),
        out_shape=out_shape,
        compiler_params=_params(("arbitrary",), 40),
        name="moe_combine_final" if final else "moe_combine",
    )(slot0, slot1, *ins)


def kernel(x_prompt, x_sample, c_prompt, c_sample, state_pool, state_gla, ada_w, ada_b, norm1_g, norm2_g, w_in, pool_w, pool_scale, gk_w2, gk_b, gla_norm_g, w_out, router_w, router_b, moe_w1, moe_w3, moe_w2, final_g):
    x = jnp.concatenate([x_prompt.reshape(N_P, D_MODEL),
                         x_sample.transpose(1, 0, 2).reshape(N_S, D_MODEL)], axis=0)
    mod = _adaln(jnp.concatenate([c_sample, c_prompt], axis=0), ada_w, ada_b)
    router_wt = router_w.T
    w_in_t = jnp.swapaxes(w_in, 1, 2)
    h = _norm_mod(x, mod, norm1_g, 0)

    pool_p, pool_s = [], []
    gla_p = gla_s = None
    for l in range(DEPTH):
        z = _matmul(h, w_in_t, l, OFF_G, "mm_in", w_transposed=True)
        log_a = _gate(h, w_in_t[l, OFF_G:, :], gk_w2, gk_b, l)

        prev_t = state_pool[l].transpose(1, 0, 2)
        mix = _pool_prompt(z, pool_w, pool_scale, l)
        mix = _pool_sample(z, prev_t, pool_w, pool_scale, mix, l)
        mix, gla_p = _gla_prompt(z, log_a, gla_norm_g, mix, gla_p, l)
        mix, gla_s = _gla_sample(z, log_a, gla_norm_g, state_gla, mix, gla_s, l)
        pool_p.append(jnp.stack([z[b * SEQ + SEQ - POOL_BUF:(b + 1) * SEQ, :D_POOL] for b in range(BATCH)]))
        u_s = z[N_P:, :D_POOL].reshape(DEC_SEQ, DEC_BATCH, D_POOL)
        pool_s.append(jnp.concatenate([prev_t[DEC_SEQ:], u_s], axis=0).transpose(1, 0, 2))

        mixed = _matmul(mix, w_out, l, D_MODEL, "mm_out")
        x1, h2, comb, mask = _post_mix(x, mixed, mod, norm2_g, router_wt, router_b, l)
        tok_of_slot, tiles, slot0, slot1, w0, w1 = _route_indices(mask, comb)
        hid = _moe_up(h2, tok_of_slot, moe_w1, moe_w3, *tiles, l)
        y = _moe_down(hid, moe_w2, *tiles, l)
        if l + 1 < DEPTH:
            x, h = _combine(y, x1, slot0, slot1, w0, w1, mod, l, norm1_g, final=False)
        else:
            y_p = _combine(y, x1, slot0, slot1, w0, w1, mod, l, final_g, final=True, tile0=0, n_tiles=N_PT)
            y_s = _combine(y, x1, slot0, slot1, w0, w1, mod, l, final_g, final=True, tile0=N_PT, n_tiles=N_ST)

    return (y_p.reshape(BATCH, SEQ, D_MODEL),
            y_s.reshape(DEC_SEQ, DEC_BATCH, D_MODEL).transpose(1, 0, 2),
            jnp.stack(pool_p), gla_p, jnp.stack(pool_s), gla_s)
```

```python
import functools

import jax
import jax.numpy as jnp
from jax import lax
from jax.experimental import pallas as pl
from jax.experimental.pallas import tpu as pltpu

F32 = jnp.float32
BF16 = jnp.bfloat16

D_MODEL = 4096
BATCH = 4
SEQ = 2048
DEPTH = 4
DEC_BATCH = 128
DEC_SEQ = 8
PAST_LEN = 16384
D_POOL = 1024
POOL_WINDOWS = (2, 4, 8, 16)
POOL_GW = 256
POOL_BUF = 15
D_GLA_V = 3072
GLA_DV = 512
GLA_HEADS = 6
GLA_DK = 256
D_GLA_K = 1536
GATE_RANK = 16
GATE_NORM = 16.0
GLA_CHUNK = 64
OFF_Q = D_POOL
OFF_K = OFF_Q + D_GLA_K
OFF_V = OFF_K + D_GLA_K
OFF_R = OFF_V + D_GLA_V
OFF_G = OFF_R + D_GLA_V
N_EXPERTS = 16
N_EXPERT_GROUPS = 4
EXPERTS_PER_GROUP = 4
TOP_K = 2
D_EXPERT = 1024
N_MOD = 6
EPS = 1e-6

N_P = BATCH * SEQ
N_S = DEC_BATCH * DEC_SEQ
N_TOK = N_P + N_S
N_COND = DEC_BATCH + BATCH

TT = 128
N_PT = N_P // TT
N_ST = N_S // TT
TILES_PER_SEQ = SEQ // TT
MM_TM = 1024
MM_TN = 512
POOL_T = 256
POOL_HIST = 16
MOE_TM = 256
MOE_TF = 512
N_SLOT_TILES = (TOP_K * N_TOK) // MOE_TM + N_EXPERTS + 1
N_SLOTS = N_SLOT_TILES * MOE_TM
GLA_SEQ_BLK = 8

VMEM_MIB = 1024 * 1024


def _params(semantics, vmem_mib):
    return pltpu.CompilerParams(dimension_semantics=semantics, vmem_limit_bytes=vmem_mib * VMEM_MIB)


def _split3(x):
    hi = x.astype(BF16)
    r = x - hi.astype(F32)
    mid = r.astype(BF16)
    lo = (r - mid.astype(F32)).astype(BF16)
    return hi, mid, lo


def _dot(a, b):
    return jnp.dot(a, b, preferred_element_type=F32)


def _dot_nt(a, b):
    return lax.dot_general(a, b, (((1,), (1,)), ((), ())), preferred_element_type=F32)


def _dot_tn(a, b):
    return lax.dot_general(a, b, (((0,), (0,)), ((), ())), preferred_element_type=F32)


def _rms(x, g):
    return x * lax.rsqrt(jnp.mean(x * x, axis=-1, keepdims=True) + EPS) * g


def _silu(x):
    return x * jax.nn.sigmoid(x)


def _by_tile_kind(i, tabs, body):
    @pl.when(i < N_PT)
    def _():
        row = DEC_BATCH + i // TILES_PER_SEQ
        body([t[pl.ds(row, 1), :] for t in tabs])

    @pl.when(i >= N_PT)
    def _():
        body([t[0:DEC_BATCH, :] for t in tabs])


def _tab_spec(layer, j):
    return pl.BlockSpec((None, N_COND, D_MODEL), lambda i, *_: (layer, 0, j))


def _row_spec(width):
    return pl.BlockSpec((TT, width), lambda i, *_: (i, 0))


def _vec_spec(layer):
    return pl.BlockSpec((None, 1, D_MODEL), lambda i, *_: (layer, 0, 0))


def _mod_kernel(c_ref, w_ref, b_ref, o_ref):
    a = _silu(c_ref[...]).astype(BF16)
    o_ref[...] = _dot(a, w_ref[...].astype(BF16)) + b_ref[...]


def _adaln(c_all, ada_w, ada_b):
    tn = 1024
    n = ada_w.shape[-1]
    return pl.pallas_call(
        _mod_kernel,
        grid=(DEPTH, n // tn),
        in_specs=[
            pl.BlockSpec((N_COND, D_MODEL), lambda l, j: (0, 0)),
            pl.BlockSpec((None, D_MODEL, tn), lambda l, j: (l, 0, j)),
            pl.BlockSpec((None, 1, tn), lambda l, j: (l, 0, j)),
        ],
        out_specs=pl.BlockSpec((None, N_COND, tn), lambda l, j: (l, 0, j)),
        out_shape=jax.ShapeDtypeStruct((DEPTH, N_COND, n), F32),
        compiler_params=_params(("arbitrary", "arbitrary"), 52),
        name="adaln",
    )(c_all, ada_w, ada_b.reshape(DEPTH, 1, n))


def _norm_mod_kernel(x_ref, g_ref, sc_ref, sh_ref, h_ref):
    def body(mods):
        sc, sh = mods
        h_ref[...] = (_rms(x_ref[...], g_ref[...]) * (1 + sc) + sh).astype(h_ref.dtype)

    _by_tile_kind(pl.program_id(0), [sc_ref, sh_ref], body)


def _norm_mod(x, mod, norm_g, layer):
    return pl.pallas_call(
        _norm_mod_kernel,
        grid=(N_PT + N_ST,),
        in_specs=[_row_spec(D_MODEL), _vec_spec(layer), _tab_spec(layer, 1), _tab_spec(layer, 0)],
        out_specs=_row_spec(D_MODEL),
        out_shape=jax.ShapeDtypeStruct((N_TOK, D_MODEL), BF16),
        compiler_params=_params(("arbitrary",), 32),
        name="norm_mod",
    )(x, norm_g.reshape(DEPTH, 1, D_MODEL), mod, mod)


HALF = D_MODEL // 2
_HI16 = 0xFFFF0000


def _pack_bf16_pairs(h):
    bits = lax.bitcast_convert_type(h.astype(BF16).astype(F32), jnp.uint32)
    return lax.shift_right_logical(bits[:, :HALF], jnp.uint32(16)) | bits[:, HALF:]


def _unpack_bf16_pairs(u):
    lo = lax.bitcast_convert_type(lax.shift_left(u, jnp.uint32(16)), F32)
    hi = lax.bitcast_convert_type(u & jnp.uint32(_HI16), F32)
    return lo.astype(BF16), hi.astype(BF16)


def _route(h2, rwt, rb, comb_ref, mask_ref):
    h_hi, h_mid, _ = _split3(h2)
    w_hi, w_mid, _ = _split3(rwt)
    logits = _dot_nt(w_hi, h_hi) + _dot_nt(w_hi, h_mid) + _dot_nt(w_mid, h_hi)
    scores = jax.nn.sigmoid(logits)
    sel = scores + rb
    a = [sel[e:e + 1] for e in range(N_EXPERTS)]
    sc = [scores[e:e + 1] for e in range(N_EXPERTS)]

    best, best_g = None, None
    for g in range(N_EXPERT_GROUPS):
        a0, a1, a2, a3 = a[4 * g:4 * g + 4]
        hi01, lo01 = jnp.maximum(a0, a1), jnp.minimum(a0, a1)
        hi23, lo23 = jnp.maximum(a2, a3), jnp.minimum(a2, a3)
        gs = jnp.maximum(hi01, hi23) + jnp.maximum(jnp.minimum(hi01, hi23), jnp.maximum(lo01, lo23))
        if g == 0:
            best, best_g = gs, jnp.zeros(gs.shape, jnp.int32)
        else:
            upd = gs > best
            best = jnp.where(upd, gs, best)
            best_g = jnp.where(upd, g, best_g)

    chosen = []
    for e in range(N_EXPERTS):
        g = e // EXPERTS_PER_GROUP
        rank = jnp.zeros(a[e].shape, jnp.int32)
        for j in range(4 * g, 4 * g + 4):
            if j == e:
                continue
            ahead = (a[j] > a[e]) | ((a[j] == a[e]) & (j < e))
            rank = rank + ahead.astype(jnp.int32)
        chosen.append((best_g == g) & (rank < TOP_K))

    denom = jnp.zeros(a[0].shape, F32)
    for e in range(N_EXPERTS):
        denom = denom + jnp.where(chosen[e], sc[e], 0.0)
    for e in range(N_EXPERTS):
        comb_ref[e:e + 1, :] = jnp.where(chosen[e], sc[e] / denom, 0.0)
        mask_ref[e:e + 1, :] = jnp.where(chosen[e], 1.0, 0.0)


def _post_mix_kernel(x_ref, m_ref, g_ref, rwt_ref, rb_ref, gate_ref, sc_ref, sh_ref,
                     x1_ref, hp_ref, comb_ref, mask_ref):
    def body(mods):
        gate, sc, sh = mods
        x1 = x_ref[...] + gate * m_ref[...]
        x1_ref[...] = x1
        h2 = _rms(x1, g_ref[...]) * (1 + sc) + sh
        hp_ref[...] = _pack_bf16_pairs(h2)
        _route(h2, rwt_ref[...], rb_ref[...], comb_ref, mask_ref)

    _by_tile_kind(pl.program_id(0), [gate_ref, sc_ref, sh_ref], body)


def _post_mix(x, mixed, mod, norm_g, router_wt, router_b, layer):
    return pl.pallas_call(
        _post_mix_kernel,
        grid=(N_PT + N_ST,),
        in_specs=[_row_spec(D_MODEL), _row_spec(D_MODEL), _vec_spec(layer),
                  pl.BlockSpec((N_EXPERTS, D_MODEL), lambda i: (0, 0)),
                  pl.BlockSpec((N_EXPERTS, 1), lambda i: (0, 0)),
                  _tab_spec(layer, 2), _tab_spec(layer, 4), _tab_spec(layer, 3)],
        out_specs=[_row_spec(D_MODEL), _row_spec(HALF),
                   pl.BlockSpec((N_EXPERTS, TT), lambda i: (0, i)),
                   pl.BlockSpec((N_EXPERTS, TT), lambda i: (0, i))],
        out_shape=[jax.ShapeDtypeStruct((N_TOK, D_MODEL), F32),
                   jax.ShapeDtypeStruct((N_TOK, HALF), jnp.uint32),
                   jax.ShapeDtypeStruct((N_EXPERTS, N_TOK), F32),
                   jax.ShapeDtypeStruct((N_EXPERTS, N_TOK), F32)],
        compiler_params=_params(("arbitrary",), 40),
        name="post_mix",
    )(x, mixed, norm_g.reshape(DEPTH, 1, D_MODEL), router_wt, router_b.reshape(N_EXPERTS, 1), mod, mod, mod)


def _mm_kernel(a_ref, w_ref, o_ref, wb_ref, *, w_transposed):
    @pl.when(pl.program_id(1) == 0)
    def _():
        w = w_ref[...]
        wb_ref[...] = (w.T if w_transposed else w).astype(BF16)

    o_ref[...] = _dot(a_ref[...], wb_ref[...]).astype(o_ref.dtype)


def _matmul(a, w, layer, n_cols, name, w_transposed=False):
    m, k = a.shape
    if w_transposed:
        w_spec = pl.BlockSpec((None, MM_TN, k), lambda j, i: (layer, j, 0))
    else:
        w_spec = pl.BlockSpec((None, k, MM_TN), lambda j, i: (layer, 0, j))
    return pl.pallas_call(
        functools.partial(_mm_kernel, w_transposed=w_transposed),
        grid=(n_cols // MM_TN, m // MM_TM),
        in_specs=[pl.BlockSpec((MM_TM, k), lambda j, i: (i, 0)), w_spec],
        out_specs=pl.BlockSpec((MM_TM, MM_TN), lambda j, i: (i, j)),
        out_shape=jax.ShapeDtypeStruct((m, n_cols), F32),
        scratch_shapes=[pltpu.VMEM((k, MM_TN), BF16)],
        compiler_params=_params(("arbitrary", "arbitrary"), 48),
        name=name,
    )(a, w)


def _gate_kernel(h_ref, wg_ref, w2_ref, b_ref, o_ref):
    wg = wg_ref[...].astype(BF16)
    z3 = _dot_nt(h_ref[...], jnp.concatenate([wg, wg, wg], axis=0))
    z_hi = z3.astype(BF16).astype(F32)
    lane = lax.broadcasted_iota(jnp.int32, z3.shape, 1)
    lhs = jnp.where(lane < 2 * GATE_RANK, z_hi, z3 - z_hi).astype(BF16)
    w_hi, w_mid, _ = _split3(w2_ref[...])
    logit = _dot(lhs, jnp.concatenate([w_hi, w_mid, w_hi], axis=0)) + b_ref[...]
    o_ref[...] = (jnp.minimum(logit, 0.0) - jnp.log(1.0 + jnp.exp(-jnp.abs(logit)))) * (1.0 / GATE_NORM)


def _gate(h, w_gate, gk_w2, gk_b, layer):
    tm = 512
    return pl.pallas_call(
        _gate_kernel,
        grid=(N_TOK // tm,),
        in_specs=[
            pl.BlockSpec((tm, D_MODEL), lambda i: (i, 0)),
            pl.BlockSpec((GATE_RANK, D_MODEL), lambda i: (0, 0)),
            pl.BlockSpec((None, GATE_RANK, D_GLA_K), lambda i: (layer, 0, 0)),
            pl.BlockSpec((None, 1, D_GLA_K), lambda i: (layer, 0, 0)),
        ],
        out_specs=pl.BlockSpec((tm, D_GLA_K), lambda i: (i, 0)),
        out_shape=jax.ShapeDtypeStruct((N_TOK, D_GLA_K), F32),
        compiler_params=_params(("arbitrary",), 32),
        name="gla_gate",
    )(h, w_gate, gk_w2, gk_b.reshape(DEPTH, 1, D_GLA_K))


def _pool_windows(load, pos, pw_ref, ps_ref, o_ref):
    for g, w in enumerate(POOL_WINDOWS):
        cs = slice(g * POOL_GW, (g + 1) * POOL_GW)
        cur = load(0, cs)
        win = cur
        for k in range(1, w):
            win = win + load(k, cs)
        cnt = jnp.minimum(pos + 1, w).astype(F32)
        d = win / cnt - cur
        y = _dot(d.astype(BF16), pw_ref[g].astype(BF16))
        o_ref[:, cs] = (y * ps_ref[:, cs]).astype(o_ref.dtype)


def _pool_prompt_kernel(u_ref, pw_ref, ps_ref, o_ref, ext_ref):
    c = pl.program_id(1)

    @pl.when(c == 0)
    def _():
        ext_ref[0:POOL_HIST, :] = jnp.zeros((POOL_HIST, D_POOL), F32)

    @pl.when(c > 0)
    def _():
        ext_ref[0:POOL_HIST, :] = ext_ref[POOL_T:POOL_T + POOL_HIST, :]

    ext_ref[POOL_HIST:POOL_HIST + POOL_T, :] = u_ref[...]
    pos = c * POOL_T + lax.broadcasted_iota(jnp.int32, (POOL_T, POOL_GW), 0)

    def load(k, cs):
        return ext_ref[POOL_HIST - k:POOL_HIST - k + POOL_T, cs]

    _pool_windows(load, pos, pw_ref, ps_ref, o_ref)


def _pool_sample_kernel(u_ref, prev_ref, pw_ref, ps_ref, _mix_in, o_ref):
    def load(k, cs):
        outs = []
        for t in range(DEC_SEQ):
            p = t - k
            if p >= 0:
                outs.append(u_ref[p * DEC_BATCH:(p + 1) * DEC_BATCH, cs])
            else:
                outs.append(prev_ref[POOL_BUF + p, :, cs])
        return jnp.concatenate(outs, axis=0)

    row = lax.broadcasted_iota(jnp.int32, (N_S, POOL_GW), 0)
    pos = PAST_LEN + lax.shift_right_logical(row, DEC_BATCH.bit_length() - 1)
    _pool_windows(load, pos, pw_ref, ps_ref, o_ref)


def _pool_prompt(z, pool_w, pool_scale, layer):
    steps = SEQ // POOL_T
    return pl.pallas_call(
        _pool_prompt_kernel,
        grid=(BATCH, steps),
        in_specs=[
            pl.BlockSpec((POOL_T, D_POOL), lambda b, c: (b * steps + c, 0)),
            pl.BlockSpec((None, len(POOL_WINDOWS), POOL_GW, POOL_GW), lambda b, c: (layer, 0, 0, 0)),
            pl.BlockSpec((None, 1, D_POOL), lambda b, c: (layer, 0, 0)),
        ],
        out_specs=pl.BlockSpec((POOL_T, D_POOL), lambda b, c: (b * steps + c, 0)),
        out_shape=jax.ShapeDtypeStruct((N_TOK, D_MODEL), BF16),
        scratch_shapes=[pltpu.VMEM((POOL_HIST + POOL_T, D_POOL), F32)],
        compiler_params=_params(("arbitrary", "arbitrary"), 32),
        name="pool_prompt",
    )(z, pool_w, pool_scale.reshape(DEPTH, 1, D_POOL))


def _pool_sample(z, prev_t, pool_w, pool_scale, mix, layer):
    blk = N_P // N_S
    return pl.pallas_call(
        _pool_sample_kernel,
        grid=(1,),
        in_specs=[
            pl.BlockSpec((N_S, D_POOL), lambda i: (blk, 0)),
            pl.BlockSpec((POOL_BUF, DEC_BATCH, D_POOL), lambda i: (0, 0, 0)),
            pl.BlockSpec((None, len(POOL_WINDOWS), POOL_GW, POOL_GW), lambda i: (layer, 0, 0, 0)),
            pl.BlockSpec((None, 1, D_POOL), lambda i: (layer, 0, 0)),
            pl.BlockSpec(memory_space=pl.ANY),
        ],
        out_specs=pl.BlockSpec((N_S, D_POOL), lambda i: (blk, 0)),
        out_shape=jax.ShapeDtypeStruct((N_TOK, D_MODEL), BF16),
        input_output_aliases={4: 0},
        compiler_params=_params(("arbitrary",), 48),
        name="pool_sample",
    )(z, prev_t, pool_w, pool_scale.reshape(DEPTH, 1, D_POOL), mix)


def _gla_chunk(la, q, k, v, r, g, s_ref, nseq, n_sub, seq_len):
    c = GLA_CHUNK
    ct = c * n_sub
    mid = seq_len // 2
    assert n_sub * nseq <= 8 and (n_sub == 1 or nseq == 1)

    def group_pos(idx):
        sub = lax.shift_right_logical(idx, c.bit_length() - 1)
        within = idx & (c - 1)
        return sub * nseq + (within & (nseq - 1)), lax.shift_right_logical(within, nseq.bit_length() - 1)

    ri = lax.broadcasted_iota(jnp.int32, (ct + 16, ct), 0)
    gj, pj = group_pos(lax.broadcasted_iota(jnp.int32, (ct + 16, ct), 1))
    gi, pi = group_pos(ri)
    extra = ri - ct
    in_sum = (((ri < ct) & (gi == gj) & (pj <= pi))
              | ((ri >= ct) & ((extra & 7) == gj) & ((extra >= 8) | (pj <= mid))))
    sums = jnp.where(in_sum, 1.0, 0.0).astype(BF16)

    la_hi, la_mid, _ = _split3(la)
    b3 = _dot(sums, la_hi) + _dot(sums, la_mid)
    b, mids, tots = b3[0:ct], b3[ct:ct + 8], b3[ct + 8:ct + 16]
    tots_col = tots.T

    def per_row(t8):
        if nseq == 1:
            return jnp.concatenate([jnp.broadcast_to(t8[s:s + 1], (c, GLA_DK)) for s in range(n_sub)], axis=0)
        return jnp.concatenate([t8] * (c // 8), axis=0)

    b_mid, b_last = per_row(mids), per_row(tots)
    gi, pi = group_pos(lax.broadcasted_iota(jnp.int32, (ct, ct), 0))
    gj, pj = group_pos(lax.broadcasted_iota(jnp.int32, (ct, ct), 1))
    row_g, _ = group_pos(lax.broadcasted_iota(jnp.int32, (ct, 1), 0))

    qs = q * (GLA_DK ** -0.5)
    q_in = qs * jnp.exp(b)
    k_out = k * jnp.exp(b_last - b)
    a = _dot_nt((qs * jnp.exp(b - b_mid)).astype(BF16), (k * jnp.exp(b_mid - b)).astype(BF16))
    a = jnp.where((gi == gj) & (pj <= pi), a, 0.0)
    if n_sub == 2:
        a = jnp.where(gi > gj, _dot_nt(q_in.astype(BF16), k_out.astype(BF16)), a)
    vb = v.astype(BF16)
    o = _dot(a.astype(BF16), vb)
    if n_sub == 2:
        st = s_ref[0]
        q_s = jnp.where(row_g == 1, q_in * jnp.exp(tots[0:1]), q_in)
        k_s = jnp.where(row_g == 0, k_out * jnp.exp(tots[1:2]), k_out)
        o = o + _dot(q_s.astype(BF16), st.astype(BF16))
        s_ref[0] = jnp.exp(tots_col[:, 0:1] + tots_col[:, 1:2]) * st + _dot_tn(k_s.astype(BF16), vb)
    else:
        q_inter = q_in.astype(BF16)
        o_inter = None
        for s in range(nseq):
            st = s_ref[s]
            oi = _dot(q_inter, st.astype(BF16))
            o_inter = oi if o_inter is None else jnp.where(row_g == s, oi, o_inter)
            ks = k_out if nseq == 1 else jnp.where(row_g == s, k_out, 0.0)
            s_ref[s] = jnp.exp(tots_col[:, s:s + 1]) * st + _dot_tn(ks.astype(BF16), vb)
        o = o + o_inter
    return _rms(o, g) * _silu(r)


GLA_SUB = 2


def _gla_prompt_kernel(q_ref, k_ref, v_ref, r_ref, la_ref, g_ref, *rest):
    o_ref, s_ref = rest[-2:]
    s_ref[...] = jnp.zeros(s_ref.shape, F32)
    n_rows = GLA_SUB * GLA_CHUNK

    def chunk(ci, carry):
        rows = pl.ds(pl.multiple_of(ci * n_rows, n_rows), n_rows)
        o = _gla_chunk(la_ref[rows, :], q_ref[rows, :], k_ref[rows, :], v_ref[rows, :], r_ref[rows, :],
                       g_ref[...], s_ref, 1, GLA_SUB, GLA_CHUNK)
        o_ref[rows, :] = o.astype(o_ref.dtype)
        return carry

    lax.fori_loop(0, SEQ // n_rows, chunk, 0, unroll=4)


def _gla_sample_kernel(q_ref, k_ref, v_ref, r_ref, la_ref, g_ref, s0_ref, *rest):
    o_ref, s_ref, o_scr = rest[-3:]
    sb = pl.program_id(1)
    s_ref[...] = s0_ref[...]
    base = pl.multiple_of(sb * GLA_SEQ_BLK, GLA_SEQ_BLK)

    def rows(ref):
        return jnp.concatenate(
            [ref[pl.ds(t * DEC_BATCH + base, GLA_SEQ_BLK), :] for t in range(DEC_SEQ)], axis=0)

    o = _gla_chunk(rows(la_ref), rows(q_ref), rows(k_ref), rows(v_ref), rows(r_ref),
                   g_ref[...], s_ref, GLA_SEQ_BLK, 1, DEC_SEQ)
    for t in range(DEC_SEQ):
        o_scr[pl.ds(t * DEC_BATCH + base, GLA_SEQ_BLK), :] = o[t * GLA_SEQ_BLK:(t + 1) * GLA_SEQ_BLK]

    @pl.when(sb == DEC_BATCH // GLA_SEQ_BLK - 1)
    def _():
        o_ref[...] = o_scr[...].astype(o_ref.dtype)


def _gla_prompt(z, log_a, norm_g, mix, s_stack, layer):
    def spec(width, off):
        return pl.BlockSpec((SEQ, width), lambda b, h: (b, off // width + h))

    ins = [z, z, z, z, log_a, norm_g.reshape(DEPTH, 1, GLA_DV), mix]
    in_specs = [spec(GLA_DK, OFF_Q), spec(GLA_DK, OFF_K), spec(GLA_DV, OFF_V), spec(GLA_DV, OFF_R),
                spec(GLA_DK, 0), pl.BlockSpec((None, 1, GLA_DV), lambda b, h: (layer, 0, 0)),
                pl.BlockSpec(memory_space=pl.ANY)]
    aliases = {6: 0}
    if s_stack is not None:
        ins.append(s_stack)
        in_specs.append(pl.BlockSpec(memory_space=pl.ANY))
        aliases[7] = 1
    return pl.pallas_call(
        _gla_prompt_kernel,
        grid=(BATCH, GLA_HEADS),
        in_specs=in_specs,
        out_specs=[
            pl.BlockSpec((SEQ, GLA_DV), lambda b, h: (b, D_POOL // GLA_DV + h)),
            pl.BlockSpec((None, 1, None, GLA_DK, GLA_DV), lambda b, h: (layer, b, h, 0, 0)),
        ],
        out_shape=[jax.ShapeDtypeStruct((N_TOK, D_MODEL), BF16),
                   jax.ShapeDtypeStruct((DEPTH, BATCH, GLA_HEADS, GLA_DK, GLA_DV), F32)],
        input_output_aliases=aliases,
        compiler_params=_params(("arbitrary", "arbitrary"), 52),
        name="gla_prompt",
    )(*ins)


def _gla_sample(z, log_a, norm_g, state_gla, mix, s_stack, layer):
    blk = N_P // N_S

    def spec(width, off):
        return pl.BlockSpec((N_S, width), lambda h, sb: (blk, off // width + h))

    ins = [z, z, z, z, log_a, norm_g.reshape(DEPTH, 1, GLA_DV), state_gla, mix]
    in_specs = [spec(GLA_DK, OFF_Q), spec(GLA_DK, OFF_K), spec(GLA_DV, OFF_V), spec(GLA_DV, OFF_R),
                spec(GLA_DK, 0), pl.BlockSpec((None, 1, GLA_DV), lambda h, sb: (layer, 0, 0)),
                pl.BlockSpec((None, GLA_SEQ_BLK, None, GLA_DK, GLA_DV), lambda h, sb: (layer, sb, h, 0, 0)),
                pl.BlockSpec(memory_space=pl.ANY)]
    aliases = {7: 0}
    if s_stack is not None:
        ins.append(s_stack)
        in_specs.append(pl.BlockSpec(memory_space=pl.ANY))
        aliases[8] = 1
    return pl.pallas_call(
        _gla_sample_kernel,
        grid=(GLA_HEADS, DEC_BATCH // GLA_SEQ_BLK),
        in_specs=in_specs,
        out_specs=[
            pl.BlockSpec((N_S, GLA_DV), lambda h, sb: (blk, D_POOL // GLA_DV + h)),
            pl.BlockSpec((None, GLA_SEQ_BLK, None, GLA_DK, GLA_DV), lambda h, sb: (layer, sb, h, 0, 0)),
        ],
        out_shape=[jax.ShapeDtypeStruct((N_TOK, D_MODEL), BF16),
                   jax.ShapeDtypeStruct((DEPTH, DEC_BATCH, GLA_HEADS, GLA_DK, GLA_DV), F32)],
        scratch_shapes=[pltpu.VMEM((N_S, GLA_DV), F32)],
        input_output_aliases=aliases,
        compiler_params=_params(("arbitrary", "arbitrary"), 52),
        name="gla_sample",
    )(*ins)


def _route_indices(mask, comb):
    mb = mask > 0.5
    cnt = jnp.sum(mb, axis=1, dtype=jnp.int32)
    tiles_e = (cnt + MOE_TM - 1) // MOE_TM
    tile_end = jnp.cumsum(tiles_e)
    tile_start = tile_end - tiles_e
    slot = tile_start[:, None] * MOE_TM + jnp.cumsum(mb, axis=1, dtype=jnp.int32) - 1
    e0 = jnp.argmax(mb, axis=0)
    e1 = N_EXPERTS - 1 - jnp.argmax(mb[::-1], axis=0)
    tok = jnp.arange(N_TOK, dtype=jnp.int32)
    slot0 = slot[e0, tok]
    slot1 = slot[e1, tok]
    w0 = comb[e0, tok][:, None]
    w1 = comb[e1, tok][:, None]
    tok_of_slot = jnp.zeros((N_SLOTS,), jnp.int32).at[slot0].set(tok).at[slot1].set(tok)
    tile = jnp.arange(N_SLOT_TILES, dtype=jnp.int32)
    tile_expert = jnp.minimum(jnp.sum(tile[:, None] >= tile_end[None, :], axis=1, dtype=jnp.int32), N_EXPERTS - 1)
    tile_valid = (tile < tile_end[-1]).astype(jnp.int32)
    prev = jnp.concatenate([jnp.full((1,), -1, jnp.int32), tile_expert[:-1]])
    tile_new = (tile_expert != prev).astype(jnp.int32)
    starts = jnp.where((tile_valid != 0) & (tile_new != 0), tile, N_SLOT_TILES)
    nxt = jnp.concatenate([lax.cummin(starts, reverse=True)[1:], jnp.full((1,), N_SLOT_TILES, jnp.int32)])
    next_expert = jnp.where(nxt < N_SLOT_TILES, tile_expert[jnp.minimum(nxt, N_SLOT_TILES - 1)], -1)
    return tok_of_slot, (tile_expert, tile_valid, tile_new, next_expert), slot0, slot1, w0, w1


def _cast_rows(src, dst, rows_per_step=256):
    def step(i, carry):
        rows = pl.ds(pl.multiple_of(i * rows_per_step, rows_per_step), rows_per_step)
        dst[rows, :] = src[rows, :].astype(dst.dtype)
        return carry

    lax.fori_loop(0, src.shape[0] // rows_per_step, step, 0)


WEIGHT_DMA_PRIORITY = 1


def _stream_expert_weights(t, cur_pass, n_pass, te_ref, tv_ref, tn_ref, nx_ref, count_ref, copies, on_arrival):
    @pl.when((cur_pass == 0) & (t == 0))
    def _():
        count_ref[0] = 0
        for c in copies(te_ref[0], 0, 0):
            c.start(priority=WEIGHT_DMA_PRIORITY)

    @pl.when((tv_ref[t] != 0) & (tn_ref[t] != 0))
    def _():
        slot = count_ref[0] % 2
        for c in copies(te_ref[t], cur_pass, slot):
            c.wait()
        on_arrival(slot)
        nxt = nx_ref[t]

        @pl.when(nxt >= 0)
        def _():
            for c in copies(nxt, cur_pass, 1 - slot):
                c.start(priority=WEIGHT_DMA_PRIORITY)

        @pl.when((nxt < 0) & (cur_pass + 1 < n_pass))
        def _():
            for c in copies(te_ref[0], cur_pass + 1, 1 - slot):
                c.start(priority=WEIGHT_DMA_PRIORITY)

        count_ref[0] = count_ref[0] + 1


def _moe_up_kernel(tok_ref, te_ref, tv_ref, tn_ref, nx_ref, h_hbm, w1_hbm, w3_hbm, o_ref, xs_hbm,
                   gbuf, xb, wbuf, w1b, w3b, gsem, xsem, wsem, count_ref, *, layer):
    f, t = pl.program_id(0), pl.program_id(1)
    slot = t % 2
    assert D_EXPERT // MOE_TF == 2

    def row_copy(tile, r, to_slot):
        tok = tok_ref[tile * MOE_TM + r]
        return pltpu.make_async_copy(h_hbm.at[pl.ds(tok, 1)], gbuf.at[to_slot, pl.ds(r, 1)], gsem.at[to_slot])

    def wait_rows(of_slot):
        pltpu.make_async_copy(h_hbm.at[pl.ds(0, MOE_TM)], gbuf.at[of_slot], gsem.at[of_slot]).wait()

    def tile_rows(tile):
        return pl.ds(pl.multiple_of(tile * MOE_TM, MOE_TM), MOE_TM)

    def save_tile(tile, of_slot):
        return pltpu.make_async_copy(xb.at[of_slot], xs_hbm.at[tile_rows(tile)], xsem.at[of_slot])

    def load_tile(tile, to_slot):
        return pltpu.make_async_copy(xs_hbm.at[tile_rows(tile)], xb.at[to_slot], xsem.at[to_slot])

    @pl.when(t == 0)
    def _():
        @pl.when(f == 0)
        def _():
            def issue(r, carry):
                row_copy(t, r, 0).start()
                return carry

            lax.fori_loop(0, MOE_TM, issue, 0, unroll=8)

        @pl.when(f != 0)
        def _():
            load_tile(t, 0).start()

    def copies(e, p, wslot):
        cols = pl.ds(pl.multiple_of(p * MOE_TF, MOE_TF), MOE_TF)
        return [pltpu.make_async_copy(w.at[layer, e, :, cols], wbuf.at[wslot, k], wsem.at[wslot, k])
                for k, w in enumerate((w1_hbm, w3_hbm))]

    def on_arrival(wslot):
        _cast_rows(wbuf.at[wslot, 0], w1b)
        _cast_rows(wbuf.at[wslot, 1], w3b)

    _stream_expert_weights(t, f, D_EXPERT // MOE_TF, te_ref, tv_ref, tn_ref, nx_ref, count_ref, copies, on_arrival)

    def compute(x):
        o_ref[...] = (_silu(_dot(x, w1b[...])) * _dot(x, w3b[...])).astype(o_ref.dtype)

    valid = tv_ref[t] != 0

    @pl.when(valid & (f == 0))
    def _():
        wait_rows(slot)
        lo, hi = _unpack_bf16_pairs(gbuf[slot])
        xb[slot, :, :HALF] = lo
        xb[slot, :, HALF:] = hi
        save_tile(t, slot).start()
        for r in range(MOE_TM):
            row_copy(t + 1, r, 1 - slot).start()
        compute(xb[slot])
        save_tile(t, slot).wait()

    @pl.when(valid & (f != 0))
    def _():
        @pl.when(tv_ref[t + 1] != 0)
        def _():
            load_tile(t + 1, 1 - slot).start()

        load_tile(t, slot).wait()
        compute(xb[slot])

    @pl.when(tv_ref[t] == 0)
    def _():
        o_ref[...] = jnp.zeros(o_ref.shape, o_ref.dtype)

        @pl.when((f == 0) & (t > 0))
        def _():
            @pl.when(tv_ref[t - 1] != 0)
            def _():
                wait_rows(slot)


def _moe_up(h2_packed, tok_of_slot, w1, w3, tile_expert, tile_valid, tile_new, next_expert, layer):
    hid, _ = pl.pallas_call(
        functools.partial(_moe_up_kernel, layer=layer),
        grid_spec=pltpu.PrefetchScalarGridSpec(
            num_scalar_prefetch=5,
            grid=(D_EXPERT // MOE_TF, N_SLOT_TILES),
            in_specs=[pl.BlockSpec(memory_space=pl.ANY)] * 3,
            out_specs=[pl.BlockSpec((MOE_TM, MOE_TF), lambda f, t, *_: (t, f)), pl.BlockSpec(memory_space=pl.ANY)],
            scratch_shapes=[pltpu.VMEM((2, MOE_TM, HALF), jnp.uint32), pltpu.VMEM((2, MOE_TM, D_MODEL), BF16),
                            pltpu.VMEM((2, 2, D_MODEL, MOE_TF), F32),
                            pltpu.VMEM((D_MODEL, MOE_TF), BF16), pltpu.VMEM((D_MODEL, MOE_TF), BF16),
                            pltpu.SemaphoreType.DMA((2,)), pltpu.SemaphoreType.DMA((2,)),
                            pltpu.SemaphoreType.DMA((2, 2)), pltpu.SMEM((1,), jnp.int32)],
        ),
        out_shape=[jax.ShapeDtypeStruct((N_SLOTS, D_EXPERT), BF16), jax.ShapeDtypeStruct((N_SLOTS, D_MODEL), BF16)],
        compiler_params=_params(("arbitrary", "arbitrary"), 56),
        name="moe_up",
    )(tok_of_slot, tile_expert, tile_valid, tile_new, next_expert, h2_packed, w1, w3)
    return hid


def _moe_down_kernel(te_ref, tv_ref, tn_ref, nx_ref, h_ref, w2_hbm, o_ref, wbuf, w2b, sem, count_ref, *, layer):
    t = pl.program_id(0)

    def copies(e, p, slot):
        del p
        return [pltpu.make_async_copy(w2_hbm.at[layer, e], wbuf.at[slot], sem.at[slot])]

    def on_arrival(slot):
        _cast_rows(wbuf.at[slot], w2b)

    _stream_expert_weights(t, 0, 1, te_ref, tv_ref, tn_ref, nx_ref, count_ref, copies, on_arrival)

    @pl.when(tv_ref[t] != 0)
    def _():
        o_ref[...] = _dot(h_ref[...], w2b[...])

    @pl.when(tv_ref[t] == 0)
    def _():
        o_ref[...] = jnp.zeros(o_ref.shape, o_ref.dtype)


def _moe_down(hid, w2, tile_expert, tile_valid, tile_new, next_expert, layer):
    return pl.pallas_call(
        functools.partial(_moe_down_kernel, layer=layer),
        grid_spec=pltpu.PrefetchScalarGridSpec(
            num_scalar_prefetch=4,
            grid=(N_SLOT_TILES,),
            in_specs=[pl.BlockSpec((MOE_TM, D_EXPERT), lambda t, *_: (t, 0)), pl.BlockSpec(memory_space=pl.ANY)],
            out_specs=pl.BlockSpec((MOE_TM, D_MODEL), lambda t, *_: (t, 0)),
            scratch_shapes=[pltpu.VMEM((2, D_EXPERT, D_MODEL), F32), pltpu.VMEM((D_EXPERT, D_MODEL), BF16),
                            pltpu.SemaphoreType.DMA((2,)), pltpu.SMEM((1,), jnp.int32)],
        ),
        out_shape=jax.ShapeDtypeStruct((N_SLOTS, D_MODEL), F32),
        compiler_params=_params(("arbitrary",), 56),
        name="moe_down",
    )(tile_expert, tile_valid, tile_new, next_expert, hid, w2)


def _combine_kernel(s0_ref, s1_ref, y_hbm, x_ref, w0_ref, w1_ref, gate_ref, g_ref, *rest,
                    tile0, n_tiles, final):
    if final:
        out_ref, bufs, sems = rest
        tabs = [gate_ref]
    else:
        sc_ref, sh_ref, x2_ref, h_ref, bufs, sems = rest
        tabs = [gate_ref, sc_ref, sh_ref]
    step = pl.program_id(0)
    i = step + tile0

    def start_tile(j):
        par = j % 2
        base = (j + tile0) * TT

        def issue(r, carry):
            for k, s_ref in enumerate((s0_ref, s1_ref)):
                pltpu.make_async_copy(y_hbm.at[pl.ds(s_ref[base + r], 1)], bufs.at[par, k, pl.ds(r, 1)],
                                      sems.at[par, k]).start()
            return carry

        lax.fori_loop(0, TT, issue, 0, unroll=8)

    @pl.when(step == 0)
    def _():
        start_tile(step)

    @pl.when(step + 1 < n_tiles)
    def _():
        start_tile(step + 1)

    par = step % 2
    for k in range(TOP_K):
        pltpu.make_async_copy(y_hbm.at[pl.ds(0, TT)], bufs.at[par, k], sems.at[par, k]).wait()

    def body(mods):
        moe = w0_ref[...] * bufs[par, 0] + w1_ref[...] * bufs[par, 1]
        x2 = x_ref[...] + mods[0] * moe
        if final:
            out_ref[...] = _rms(x2, g_ref[...])
        else:
            x2_ref[...] = x2
            h_ref[...] = (_rms(x2, g_ref[...]) * (1 + mods[1]) + mods[2]).astype(h_ref.dtype)

    _by_tile_kind(i, tabs, body)


def _combine(y, x1, slot0, slot1, w0, w1, mod, layer, norm_g, *, final, tile0=0, n_tiles=N_PT + N_ST):
    def rows(width):
        return pl.BlockSpec((TT, width), lambda i, *_: (i + tile0, 0))

    in_specs = [pl.BlockSpec(memory_space=pl.ANY), rows(D_MODEL), rows(1), rows(1), _tab_spec(layer, 5)]
    ins = [y, x1, w0, w1, mod]
    if final:
        in_specs.append(pl.BlockSpec((1, D_MODEL), lambda i, *_: (0, 0)))
        ins.append(norm_g.reshape(1, D_MODEL))
        out_specs = _row_spec(D_MODEL)
        out_shape = jax.ShapeDtypeStruct((n_tiles * TT, D_MODEL), F32)
    else:
        in_specs += [_vec_spec(layer + 1), _tab_spec(layer + 1, 1), _tab_spec(layer + 1, 0)]
        ins += [norm_g.reshape(DEPTH, 1, D_MODEL), mod, mod]
        out_specs = [_row_spec(D_MODEL), _row_spec(D_MODEL)]
        out_shape = [jax.ShapeDtypeStruct((N_TOK, D_MODEL), F32), jax.ShapeDtypeStruct((N_TOK, D_MODEL), BF16)]
    return pl.pallas_call(
        functools.partial(_combine_kernel, tile0=tile0, n_tiles=n_tiles, final=final),
        grid_spec=pltpu.PrefetchScalarGridSpec(
            num_scalar_prefetch=2,
            grid=(n_tiles,),
            in_specs=in_specs,
            out_specs=out_specs,
            scratch_shapes=[pltpu.VMEM((2, TOP_K, TT, D_MODEL), F32), pltpu.SemaphoreType.DMA((2, TOP_K))],
        ),
        out_shape=out_shape,
        compiler_params=_params(("arbitrary",), 40),
        name="moe_combine_final" if final else "moe_combine",
    )(slot0, slot1, *ins)


def kernel(x_prompt, x_sample, c_prompt, c_sample, state_pool, state_gla, ada_w, ada_b, norm1_g, norm2_g, w_in, pool_w, pool_scale, gk_w2, gk_b, gla_norm_g, w_out, router_w, router_b, moe_w1, moe_w3, moe_w2, final_g):
    x = jnp.concatenate([x_prompt.reshape(N_P, D_MODEL),
                         x_sample.transpose(1, 0, 2).reshape(N_S, D_MODEL)], axis=0)
    mod = _adaln(jnp.concatenate([c_sample, c_prompt], axis=0), ada_w, ada_b)
    router_wt = router_w.T
    w_in_t = jnp.swapaxes(w_in, 1, 2)
    h = _norm_mod(x, mod, norm1_g, 0)

    pool_p, pool_s = [], []
    gla_p = gla_s = None
    for l in range(DEPTH):
        z = _matmul(h, w_in_t, l, OFF_G, "mm_in", w_transposed=True)
        log_a = _gate(h, w_in_t[l, OFF_G:, :], gk_w2, gk_b, l)

        prev_t = state_pool[l].transpose(1, 0, 2)
        mix = _pool_prompt(z, pool_w, pool_scale, l)
        mix = _pool_sample(z, prev_t, pool_w, pool_scale, mix, l)
        mix, gla_p = _gla_prompt(z, log_a, gla_norm_g, mix, gla_p, l)
        mix, gla_s = _gla_sample(z, log_a, gla_norm_g, state_gla, mix, gla_s, l)
        pool_p.append(jnp.stack([z[b * SEQ + SEQ - POOL_BUF:(b + 1) * SEQ, :D_POOL] for b in range(BATCH)]))
        u_s = z[N_P:, :D_POOL].reshape(DEC_SEQ, DEC_BATCH, D_POOL)
        pool_s.append(jnp.concatenate([prev_t[DEC_SEQ:], u_s], axis=0).transpose(1, 0, 2))

        mixed = _matmul(mix, w_out, l, D_MODEL, "mm_out")
        x1, h2, comb, mask = _post_mix(x, mixed, mod, norm2_g, router_wt, router_b, l)
        tok_of_slot, tiles, slot0, slot1, w0, w1 = _route_indices(mask, comb)
        hid = _moe_up(h2, tok_of_slot, moe_w1, moe_w3, *tiles, l)
        y = _moe_down(hid, moe_w2, *tiles, l)
        if l + 1 < DEPTH:
            x, h = _combine(y, x1, slot0, slot1, w0, w1, mod, l, norm1_g, final=False)
        else:
            y_p = _combine(y, x1, slot0, slot1, w0, w1, mod, l, final_g, final=True, tile0=0, n_tiles=N_PT)
            y_s = _combine(y, x1, slot0, slot1, w0, w1, mod, l, final_g, final=True, tile0=N_PT, n_tiles=N_ST)

    return (y_p.reshape(BATCH, SEQ, D_MODEL),
            y_s.reshape(DEC_SEQ, DEC_BATCH, D_MODEL).transpose(1, 0, 2),
            jnp.stack(pool_p), gla_p, jnp.stack(pool_s), gla_s)
```
